```python
import jax, jax.numpy as jnp
from jax import lax
import numpy as np

D_MODEL = 1024
BATCH = 2
SEQ = 16384
DEPTH = 1
DEC_BATCH = 8
DEC_SEQ = 64
PAST_LEN = 1024

CHUNK = 64
HEAD_DIM = 64
WIN_HEADS = 8
WIN_KV_HEADS = 2
WINDOW = 128
WIN_CHUNKS = WINDOW // CHUNK
BAND_HEADS = 8
BAND_CHUNKS = 8
BAND = BAND_CHUNKS * CHUNK
REL_CLIP = 128
PLE_DIM = 256
ALIBI_MAX = 8.0
EPS = 1e-6
NEG_INF = -1e30

WIN_WIDTH = WIN_HEADS * HEAD_DIM
WIN_KV_WIDTH = WIN_KV_HEADS * HEAD_DIM
BAND_WIDTH = BAND_HEADS * HEAD_DIM
IN_SPLITS = (WIN_WIDTH, WIN_KV_WIDTH, WIN_KV_WIDTH, WIN_WIDTH,
             BAND_WIDTH, BAND_WIDTH, BAND_WIDTH, BAND_WIDTH, D_MODEL, D_MODEL)
IN_COLS = sum(IN_SPLITS)

kernel_name = "hybrid_streaming_encoder_step"


def rms_norm(x, g):
    xf = x.astype(jnp.float32)
    y = xf * lax.rsqrt(jnp.mean(xf * xf, axis=-1, keepdims=True) + EPS)
    return (y * g.astype(jnp.float32)).astype(x.dtype)


def project(h, g_in, w_in, g_q_win, g_k_win, g_q_band, g_k_band):
    B, S, _ = h.shape
    u = rms_norm(h, g_in) @ w_in
    cuts = np.cumsum(IN_SPLITS)[:-1].tolist()
    qa, ka, va, za, qb, kb, vb, zb, ga, gb = jnp.split(u, cuts, axis=-1)
    heads = lambda t, n: t.reshape(B, S, n, HEAD_DIM)
    qa = rms_norm(heads(qa, WIN_HEADS), g_q_win)
    ka = rms_norm(heads(ka, WIN_KV_HEADS), g_k_win)
    va = heads(va, WIN_KV_HEADS)
    qb = rms_norm(heads(qb, BAND_HEADS), g_q_band)
    kb = rms_norm(heads(kb, BAND_HEADS), g_k_band)
    vb = heads(vb, BAND_HEADS)
    return qa, ka, va, za, qb, kb, vb, zb, ga, gb


def rel_distance(past, cq, kn):
    return jnp.arange(cq)[:, None] + past - jnp.arange(kn)[None, :]


def alibi_bias(past, cq, kn):
    dist = jnp.abs(rel_distance(past, cq, kn)).astype(jnp.float32)
    slopes = jnp.exp2(-ALIBI_MAX * jnp.arange(1, WIN_HEADS + 1, dtype=jnp.float32) / WIN_HEADS)
    return -slopes[:, None, None] * dist[None]


def rel_pos_bias(table, past, cq, kn):
    idx = jnp.clip(rel_distance(past, cq, kn), -REL_CLIP, REL_CLIP) + REL_CLIP
    return table.astype(jnp.float32)[:, idx]


def band_attention(q, k, v, bias, valid, sink):
    B, nC, Cq, H, dh = q.shape
    Kn, Hkv = k.shape[2], k.shape[3]
    G = H // Hkv
    qg = q.reshape(B, nC, Cq, Hkv, G, dh)
    s = jnp.einsum('bnqhgd,bnkhd->bnhgqk', qg, k, preferred_element_type=jnp.float32) * (dh ** -0.5)
    s = s + bias.reshape(Hkv, G, Cq, Kn)
    if valid is not None:
        s = jnp.where(valid[None, :, None, None, None, :], s, NEG_INF)
    if sink is not None:
        sk = sink.astype(jnp.float32).reshape(1, 1, Hkv, G, 1, 1)
        m = jnp.maximum(jnp.max(s, axis=-1, keepdims=True), sk)
        e = jnp.exp(s - m)
        pr = e / (jnp.sum(e, axis=-1, keepdims=True) + jnp.exp(sk - m))
    else:
        pr = jax.nn.softmax(s, axis=-1)
    o = jnp.einsum('bnhgqk,bnkhd->bnqhgd', pr.astype(v.dtype), v)
    return o.reshape(B, nC, Cq, H * dh)


def chunk_band(t, nb):
    B, S, Hk, dh = t.shape
    nC = S // CHUNK
    tp = jnp.pad(t.reshape(B, nC, CHUNK, Hk, dh), ((0, 0), (nb, 0), (0, 0), (0, 0), (0, 0)))
    return jnp.concatenate([tp[:, j:j + nC] for j in range(nb + 1)], axis=2)


def band_valid(nC, nb):
    ci = jnp.arange(nC)[:, None] - nb + jnp.arange(nb + 1)[None, :]
    return jnp.repeat(ci >= 0, CHUNK, axis=1)


def merge_and_finish(h, oa, ob, za, zb, ga, gb, p, w_o_win, w_o_band, w_out, g_ple, w_ple_gate, w_ple):
    branch_a = (oa * jax.nn.silu(za)) @ w_o_win
    branch_b = (ob * jax.nn.silu(zb)) @ w_o_band
    h = h + (jax.nn.sigmoid(ga) * branch_a + jax.nn.sigmoid(gb) * branch_b) @ w_out
    gate = jax.nn.sigmoid(rms_norm(h, g_ple) @ w_ple_gate)
    return h + gate * (p @ w_ple)


def prompt_layer(h, p, lw):
    (g_in, w_in, g_q_win, g_k_win, sink_win, g_q_band, g_k_band, rel_bias_band,
     w_o_win, w_o_band, w_out, g_ple, w_ple_gate, w_ple) = lw
    B, S, _ = h.shape
    nC = S // CHUNK
    qa, ka, va, za, qb, kb, vb, zb, ga, gb = project(h, g_in, w_in, g_q_win, g_k_win, g_q_band, g_k_band)
    kn_a = (WIN_CHUNKS + 1) * CHUNK
    oa = band_attention(qa.reshape(B, nC, CHUNK, WIN_HEADS, HEAD_DIM),
                        chunk_band(ka, WIN_CHUNKS), chunk_band(va, WIN_CHUNKS),
                        alibi_bias(WIN_CHUNKS * CHUNK, CHUNK, kn_a),
                        band_valid(nC, WIN_CHUNKS), sink_win).reshape(B, S, WIN_WIDTH)
    kn_b = (BAND_CHUNKS + 1) * CHUNK
    ob = band_attention(qb.reshape(B, nC, CHUNK, BAND_HEADS, HEAD_DIM),
                        chunk_band(kb, BAND_CHUNKS), chunk_band(vb, BAND_CHUNKS),
                        rel_pos_bias(rel_bias_band, BAND_CHUNKS * CHUNK, CHUNK, kn_b),
                        band_valid(nC, BAND_CHUNKS), None).reshape(B, S, BAND_WIDTH)
    out = merge_and_finish(h, oa, ob, za, zb, ga, gb, p, w_o_win, w_o_band, w_out, g_ple, w_ple_gate, w_ple)
    la, lb = min(WINDOW, S), min(BAND, S)
    return out, ka[:, S - la:], va[:, S - la:], kb[:, S - lb:], vb[:, S - lb:]


def sample_layer(h, p, ck_win, cv_win, ck_band, cv_band, lw):
    (g_in, w_in, g_q_win, g_k_win, sink_win, g_q_band, g_k_band, rel_bias_band,
     w_o_win, w_o_band, w_out, g_ple, w_ple_gate, w_ple) = lw
    B, S, _ = h.shape
    qa, ka, va, za, qb, kb, vb, zb, ga, gb = project(h, g_in, w_in, g_q_win, g_k_win, g_q_band, g_k_band)
    ka_all = jnp.concatenate([ck_win, ka], axis=1)
    va_all = jnp.concatenate([cv_win, va], axis=1)
    kb_all = jnp.concatenate([ck_band, kb], axis=1)
    vb_all = jnp.concatenate([cv_band, vb], axis=1)
    la, lb = ck_win.shape[1], ck_band.shape[1]
    oa = band_attention(qa[:, None], ka_all[:, None], va_all[:, None],
                        alibi_bias(la, S, la + S), None, sink_win)[:, 0]
    ob = band_attention(qb[:, None], kb_all[:, None], vb_all[:, None],
                        rel_pos_bias(rel_bias_band, lb, S, lb + S), None, None)[:, 0]
    out = merge_and_finish(h, oa, ob, za, zb, ga, gb, p, w_o_win, w_o_band, w_out, g_ple, w_ple_gate, w_ple)
    na, nb = ka_all.shape[1], kb_all.shape[1]
    ra, rb = min(WINDOW, na), min(BAND, nb)
    return out, ka_all[:, na - ra:], va_all[:, na - ra:], kb_all[:, nb - rb:], vb_all[:, nb - rb:]


def setup_inputs(seed: int = 0) -> dict:
    key = jax.random.key(seed)
    ks = jax.random.split(key, 22)
    nrm = lambda k, shape, s=1.0: s * jax.random.normal(k, shape, jnp.float32)
    la, lb = min(WINDOW, PAST_LEN), min(BAND, PAST_LEN)
    return {
        "x_prompt": nrm(ks[0], (BATCH, SEQ, D_MODEL)),
        "x_sample": nrm(ks[1], (DEC_BATCH, DEC_SEQ, D_MODEL)),
        "cache_k_win": nrm(ks[2], (DEPTH, DEC_BATCH, la, WIN_KV_HEADS, HEAD_DIM)),
        "cache_v_win": nrm(ks[3], (DEPTH, DEC_BATCH, la, WIN_KV_HEADS, HEAD_DIM)),
        "cache_k_band": nrm(ks[4], (DEPTH, DEC_BATCH, lb, BAND_HEADS, HEAD_DIM)),
        "cache_v_band": nrm(ks[5], (DEPTH, DEC_BATCH, lb, BAND_HEADS, HEAD_DIM)),
        "p_prompt": nrm(ks[6], (DEPTH, BATCH, SEQ, PLE_DIM)),
        "p_sample": nrm(ks[7], (DEPTH, DEC_BATCH, DEC_SEQ, PLE_DIM)),
        "g_in": 1.0 + nrm(ks[8], (DEPTH, D_MODEL), 0.05),
        "w_in": nrm(ks[9], (DEPTH, D_MODEL, IN_COLS), D_MODEL ** -0.5),
        "g_q_win": 1.0 + nrm(ks[10], (DEPTH, HEAD_DIM), 0.05),
        "g_k_win": 1.0 + nrm(ks[11], (DEPTH, HEAD_DIM), 0.05),
        "sink_win": nrm(ks[12], (DEPTH, WIN_HEADS), 0.5),
        "g_q_band": 1.0 + nrm(ks[13], (DEPTH, HEAD_DIM), 0.05),
        "g_k_band": 1.0 + nrm(ks[14], (DEPTH, HEAD_DIM), 0.05),
        "rel_bias_band": nrm(ks[15], (DEPTH, BAND_HEADS, 2 * REL_CLIP + 1), 0.1),
        "w_o_win": nrm(ks[16], (DEPTH, WIN_WIDTH, D_MODEL), WIN_WIDTH ** -0.5),
        "w_o_band": nrm(ks[17], (DEPTH, BAND_WIDTH, D_MODEL), BAND_WIDTH ** -0.5),
        "w_out": nrm(ks[18], (DEPTH, D_MODEL, D_MODEL), D_MODEL ** -0.5),
        "g_ple": 1.0 + nrm(ks[19], (DEPTH, D_MODEL), 0.05),
        "w_ple_gate": nrm(ks[20], (DEPTH, D_MODEL, D_MODEL), D_MODEL ** -0.5),
        "w_ple": nrm(ks[21], (DEPTH, PLE_DIM, D_MODEL), PLE_DIM ** -0.5),
    }


def reference(x_prompt, x_sample, cache_k_win, cache_v_win, cache_k_band, cache_v_band,
              p_prompt, p_sample, g_in, w_in, g_q_win, g_k_win, sink_win, g_q_band, g_k_band,
              rel_bias_band, w_o_win, w_o_band, w_out, g_ple, w_ple_gate, w_ple):
    h_p, h_s = x_prompt, x_sample
    kwp, vwp, kbp, vbp, kws, vws, kbs, vbs = [], [], [], [], [], [], [], []
    for i in range(DEPTH):
        lw = (g_in[i], w_in[i], g_q_win[i], g_k_win[i], sink_win[i], g_q_band[i], g_k_band[i],
              rel_bias_band[i], w_o_win[i], w_o_band[i], w_out[i], g_ple[i], w_ple_gate[i], w_ple[i])
        h_p, a, b, c, d = prompt_layer(h_p, p_prompt[i], lw)
        kwp.append(a); vwp.append(b); kbp.append(c); vbp.append(d)
        h_s, a, b, c, d = sample_layer(h_s, p_sample[i], cache_k_win[i], cache_v_win[i],
                                       cache_k_band[i], cache_v_band[i], lw)
        kws.append(a); vws.append(b); kbs.append(c); vbs.append(d)
    return (h_p, h_s,
            jnp.stack(kwp), jnp.stack(vwp), jnp.stack(kbp), jnp.stack(vbp),
            jnp.stack(kws), jnp.stack(vws), jnp.stack(kbs), jnp.stack(vbs))
```

```python
import functools

import jax
import jax.numpy as jnp
from jax import lax
from jax.experimental import pallas as pl
from jax.experimental.pallas import tpu as pltpu

CHUNK = 64
HEAD_DIM = 64
WIN_HEADS = 8
WIN_KV_HEADS = 2
WIN_CHUNKS = 2
BAND_HEADS = 8
BAND_CHUNKS = 8
REL_CLIP = 128
ALIBI_MAX = 8.0
EPS = 1e-6
NEG_INF = -1e30

LANES = 128
PROJ_TILE = 512
ATTN_TILE = 256
VMEM_LIMIT_BYTES = 56 * 1024 * 1024

_F32 = jnp.float32
_BF16 = jnp.bfloat16
_NT = (((1,), (1,)), ((), ()))


def _rmsnorm(x, g):
    ms = jnp.mean(x * x, axis=-1, keepdims=True)
    return x * lax.rsqrt(ms + EPS) * g


def _lo_lanes(rows=1):
    return lax.broadcasted_iota(jnp.int32, (rows, LANES), 1) < HEAD_DIM


def _head_rmsnorm(u, g_pair):
    lo = _lo_lanes()
    outs = []
    for j in range(u.shape[1] // LANES):
        t = u[:, j * LANES:(j + 1) * LANES]
        sq = t * t
        s_lo = jnp.sum(jnp.where(lo, sq, 0.0), axis=-1, keepdims=True)
        s_hi = jnp.sum(jnp.where(lo, 0.0, sq), axis=-1, keepdims=True)
        ms = jnp.where(lo, s_lo, s_hi) * (1.0 / HEAD_DIM)
        outs.append(t * lax.rsqrt(ms + EPS) * g_pair)
    return outs[0] if len(outs) == 1 else jnp.concatenate(outs, axis=-1)


def _dup_heads(t):
    lo = _lo_lanes()
    r = pltpu.roll(t, HEAD_DIM, axis=1)
    return jnp.concatenate([jnp.where(lo, t, r), jnp.where(lo, r, t)], axis=-1)


_C_QA, _C_KA, _C_VA, _C_ZA, _C_QB, _C_KB, _C_VB, _C_ZB, _C_GA, _C_GB, _C_END = (
    0, 512, 640, 768, 1280, 1792, 2304, 2816, 3328, 4352, 5376)


def _proj_kernel(x_ref, gin_ref, w_ref, gqa_ref, gka_ref, gqb_ref, gkb_ref,
                 qa_ref, ka_ref, va_ref, za_ref, qb_ref, kb_ref, vb_ref, zb_ref, ga_ref, gb_ref,
                 kat_ref, vat_ref, kbt_ref, vbt_ref):
    last = pl.program_id(1) == pl.num_programs(1) - 1
    xn = _rmsnorm(x_ref[0], gin_ref[...]).astype(_BF16)

    def proj(a, b):
        return jnp.dot(xn, w_ref[:, a:b], preferred_element_type=_F32)

    scale = HEAD_DIM ** -0.5
    qa_ref[0] = (_head_rmsnorm(proj(_C_QA, _C_KA), gqa_ref[...]) * scale).astype(_BF16)
    qb_ref[0] = (_head_rmsnorm(proj(_C_QB, _C_KB), gqb_ref[...]) * scale).astype(_BF16)

    ka = _head_rmsnorm(proj(_C_KA, _C_VA), gka_ref[...])
    va = proj(_C_VA, _C_ZA)
    ka_ref[0] = _dup_heads(ka).astype(_BF16)
    va_ref[0] = _dup_heads(va).astype(_BF16)
    kb = _head_rmsnorm(proj(_C_KB, _C_VB), gkb_ref[...])
    vb = proj(_C_VB, _C_ZB)
    kb_ref[0] = kb.astype(_BF16)
    vb_ref[0] = vb.astype(_BF16)

    za_ref[0] = proj(_C_ZA, _C_QB).astype(_BF16)
    zb_ref[0] = proj(_C_ZB, _C_GA).astype(_BF16)
    ga_ref[0] = proj(_C_GA, _C_GB).astype(_BF16)
    gb_ref[0] = proj(_C_GB, _C_END).astype(_BF16)

    @pl.when(last)
    def _():
        ta, tb = kat_ref.shape[1], kbt_ref.shape[1]
        t = ka.shape[0]
        kat_ref[0] = ka[t - ta:, :]
        vat_ref[0] = va[t - ta:, :]
        kbt_ref[0] = kb[t - tb:, :]
        vbt_ref[0] = vb[t - tb:, :]


def _const_spec(shape):
    return pl.BlockSpec(shape, lambda *_: (0,) * len(shape), pipeline_mode=pl.Buffered(1))


def _project(x, gin, w_in, gqa, gka, gqb, gkb, *, tile, tail_a, tail_b):
    bsz, seq, d = x.shape
    nt = seq // tile
    row = lambda w: pl.BlockSpec((1, tile, w), lambda b, t: (b, t, 0))
    tail = lambda r, w: pl.BlockSpec((1, r, w), lambda b, t: (b, 0, 0))
    act = lambda w: jax.ShapeDtypeStruct((bsz, seq, w), _BF16)
    widths = (512, 256, 256, 512, 512, 512, 512, 512, 1024, 1024)
    return pl.pallas_call(
        _proj_kernel,
        grid=(bsz, nt),
        in_specs=[row(d), _const_spec((1, d)), _const_spec(w_in.shape),
                  _const_spec((1, LANES)), _const_spec((1, LANES)),
                  _const_spec((1, LANES)), _const_spec((1, LANES))],
        out_specs=[row(w) for w in widths] + [tail(tail_a, 128), tail(tail_a, 128),
                                              tail(tail_b, 512), tail(tail_b, 512)],
        out_shape=[act(w) for w in widths] + [
            jax.ShapeDtypeStruct((bsz, tail_a, 128), _F32),
            jax.ShapeDtypeStruct((bsz, tail_a, 128), _F32),
            jax.ShapeDtypeStruct((bsz, tail_b, 512), _F32),
            jax.ShapeDtypeStruct((bsz, tail_b, 512), _F32)],
        compiler_params=pltpu.CompilerParams(
            dimension_semantics=("arbitrary", "arbitrary"),
            vmem_limit_bytes=VMEM_LIMIT_BYTES),
        name="proj",
    )(x, gin, w_in, gqa, gka, gqb, gkb)


def _mixer(q, k_pieces, v_pieces, bias_fn, sink_fn, n_heads, kv_col):
    lo = _lo_lanes()
    outs = []
    for p in range(n_heads // 2):
        qp = q[:, p * LANES:(p + 1) * LANES]
        halves = []
        for half in range(2):
            h = 2 * p + half
            qm = jnp.where(lo if half == 0 else jnp.logical_not(lo), qp, jnp.zeros_like(qp))
            c0 = kv_col(h)
            s = [lax.dot_general(qm, kp[:, c0:c0 + LANES], _NT, preferred_element_type=_F32)
                 + bias_fn(h, j) for j, kp in enumerate(k_pieces)]
            m = functools.reduce(jnp.maximum, [jnp.max(sj, axis=-1, keepdims=True) for sj in s])
            if sink_fn is not None:
                sk = sink_fn(h)
                m = jnp.maximum(m, sk)
            e = [jnp.exp(sj - m) for sj in s]
            den = functools.reduce(jnp.add, [jnp.sum(ej, axis=-1, keepdims=True) for ej in e])
            if sink_fn is not None:
                den = den + jnp.exp(sk - m)
            o = functools.reduce(jnp.add, [
                jnp.dot(ej.astype(_BF16), vp[:, c0:c0 + LANES], preferred_element_type=_F32)
                for ej, vp in zip(e, v_pieces)])
            halves.append(o / den)
        outs.append(jnp.where(lo, halves[0], halves[1]))
    return jnp.concatenate(outs, axis=-1)


def _alibi_pieces(rows, piece_rows, chunk_mask):
    out, c_off = [], 0
    for n in piece_rows:
        r = lax.broadcasted_iota(jnp.int32, (rows, n), 0)
        c = lax.broadcasted_iota(jnp.int32, (rows, n), 1) + c_off
        dist = jnp.abs(r + WIN_CHUNKS * CHUNK - c).astype(_F32)
        valid = None
        if chunk_mask:
            qc, kc = r // CHUNK, c // CHUNK
            valid = jnp.logical_and(kc >= qc, kc <= qc + WIN_CHUNKS)
        out.append((dist, valid))
        c_off += n
    return out


def _layer_body(qa, qb, ka_p, va_p, kb_p, vb_p, alibi, a_pen, bias_b_fn, sink_ref,
                za, zb, ga, gb, x, p, woa_ref, wob_ref, wout_ref, gple_ref, wpg_ref, wple_ref):
    def bias_a(h, j):
        dist, valid = alibi[j]
        slope = 2.0 ** (-ALIBI_MAX * (h + 1) / WIN_HEADS)
        b = dist * (-slope)
        if valid is not None:
            b = jnp.where(valid, b, NEG_INF)
        if a_pen[j] is not None:
            b = b + a_pen[j]
        return b

    oa = _mixer(qa, ka_p, va_p, bias_a, lambda h: sink_ref[h], WIN_HEADS,
                lambda h: LANES * (h // (WIN_HEADS // WIN_KV_HEADS)))
    ob = _mixer(qb, kb_p, vb_p, bias_b_fn, None, BAND_HEADS, lambda h: LANES * (h // 2))

    za = za.astype(_F32)
    zb = zb.astype(_F32)
    br_a = jnp.dot((oa * jax.nn.silu(za)).astype(_BF16), woa_ref[...], preferred_element_type=_F32)
    br_b = jnp.dot((ob * jax.nn.silu(zb)).astype(_BF16), wob_ref[...], preferred_element_type=_F32)
    mrg = jax.nn.sigmoid(ga.astype(_F32)) * br_a + jax.nn.sigmoid(gb.astype(_F32)) * br_b
    h2 = x + jnp.dot(mrg.astype(_BF16), wout_ref[...], preferred_element_type=_F32)
    gate = jax.nn.sigmoid(jnp.dot(_rmsnorm(h2, gple_ref[...]).astype(_BF16), wpg_ref[...],
                                  preferred_element_type=_F32))
    return h2 + gate * jnp.dot(p.astype(_BF16), wple_ref[...], preferred_element_type=_F32)


def _prompt_attn_kernel(qa_ref, qb_ref, kah_ref, kac_ref, vah_ref, vac_ref,
                        kb2_ref, kb1_ref, kb0_ref, vb2_ref, vb1_ref, vb0_ref,
                        za_ref, zb_ref, ga_ref, gb_ref, x_ref, p_ref, biasb_ref, sink_ref,
                        woa_ref, wob_ref, wout_ref, gple_ref, wpg_ref, wple_ref, o_ref):
    i = pl.program_id(1)
    rows = qa_ref.shape[1]
    pen = lambda need: jnp.where(i >= need, 0.0, NEG_INF).astype(_F32)
    alibi = _alibi_pieces(rows, (kah_ref.shape[1], kac_ref.shape[1]), True)
    b_pen = (pen(2), pen(1), None)

    def bias_b(h, j):
        b = biasb_ref[h, :, j * rows:(j + 1) * rows]
        return b if b_pen[j] is None else b + b_pen[j]

    o_ref[0] = _layer_body(
        qa_ref[0], qb_ref[0], (kah_ref[0], kac_ref[0]), (vah_ref[0], vac_ref[0]),
        (kb2_ref[0], kb1_ref[0], kb0_ref[0]), (vb2_ref[0], vb1_ref[0], vb0_ref[0]),
        alibi, (pen(1), None), bias_b, sink_ref,
        za_ref[0], zb_ref[0], ga_ref[0], gb_ref[0], x_ref[0], p_ref[0],
        woa_ref, wob_ref, wout_ref, gple_ref, wpg_ref, wple_ref)


def _sample_attn_kernel(qa_ref, qb_ref, kac_ref, kan_ref, vac_ref, van_ref,
                        kbc_ref, kbn_ref, vbc_ref, vbn_ref,
                        za_ref, zb_ref, ga_ref, gb_ref, x_ref, p_ref, biasb_ref, sink_ref,
                        woa_ref, wob_ref, wout_ref, gple_ref, wpg_ref, wple_ref, o_ref):
    rows = qa_ref.shape[1]
    half = kbc_ref.shape[1] // 2
    alibi = _alibi_pieces(rows, (kac_ref.shape[1], kan_ref.shape[1]), False)
    edges = (0, half, 2 * half, 2 * half + rows)

    def bias_b(h, j):
        return biasb_ref[h, :, edges[j]:edges[j + 1]]

    kbc, vbc = kbc_ref[0], vbc_ref[0]
    o_ref[0] = _layer_body(
        qa_ref[0], qb_ref[0], (kac_ref[0], kan_ref[0]), (vac_ref[0], van_ref[0]),
        (kbc[:half], kbc[half:], kbn_ref[0]), (vbc[:half], vbc[half:], vbn_ref[0]),
        alibi, (None, None), bias_b, sink_ref,
        za_ref[0], zb_ref[0], ga_ref[0], gb_ref[0], x_ref[0], p_ref[0],
        woa_ref, wob_ref, wout_ref, gple_ref, wpg_ref, wple_ref)


def _band_bias_table(table, rows):
    band = BAND_CHUNKS * CHUNK
    r = jnp.arange(rows)[:, None]
    c = jnp.arange(rows + band)[None, :]
    idx = jnp.clip(r + band - c, -REL_CLIP, REL_CLIP) + REL_CLIP
    qc, kc = r // CHUNK, c // CHUNK
    valid = (kc >= qc) & (kc <= qc + BAND_CHUNKS)
    return jnp.where(valid[None], table.astype(_F32)[:, idx], NEG_INF)


def _weight_specs(d, ple):
    return [_const_spec((512, d)), _const_spec((512, d)), _const_spec((d, d)),
            _const_spec((1, d)), _const_spec((d, d)), _const_spec((ple, d))]


def _prompt_attend(acts, x, p, biasb, sink, weights):
    qa, ka, va, za, qb, kb, vb, zb, ga, gb = acts
    bsz, seq, d = x.shape
    tq = ATTN_TILE
    cur = lambda w: pl.BlockSpec((1, tq, w), lambda b, i: (b, i, 0))
    prev = lambda w, n: pl.BlockSpec((1, tq, w), lambda b, i: (b, jnp.maximum(i - n, 0), 0))
    halo_rows = WIN_CHUNKS * CHUNK
    halo = pl.BlockSpec((1, halo_rows, 256),
                        lambda b, i: (b, jnp.maximum(i * (tq // halo_rows) - 1, 0), 0))
    return pl.pallas_call(
        _prompt_attn_kernel,
        grid=(bsz, seq // tq),
        in_specs=[cur(512), cur(512), halo, cur(256), halo, cur(256),
                  prev(512, 2), prev(512, 1), cur(512), prev(512, 2), prev(512, 1), cur(512),
                  cur(512), cur(512), cur(d), cur(d), cur(d), cur(p.shape[-1]),
                  _const_spec(biasb.shape), pl.BlockSpec(memory_space=pltpu.SMEM)]
                 + _weight_specs(d, p.shape[-1]),
        out_specs=cur(d),
        out_shape=jax.ShapeDtypeStruct((bsz, seq, d), _F32),
        compiler_params=pltpu.CompilerParams(
            dimension_semantics=("arbitrary", "arbitrary"),
            vmem_limit_bytes=VMEM_LIMIT_BYTES),
        name="prompt_attn",
    )(qa, qb, ka, ka, va, va, kb, kb, kb, vb, vb, vb, za, zb, ga, gb, x, p, biasb, sink, *weights)


def _sample_attend(acts, caches, x, p, biasb, sink, weights):
    qa, ka, va, za, qb, kb, vb, zb, ga, gb = acts
    cka, cva, ckb, cvb = caches
    bsz, seq, d = x.shape
    blk = lambda a: pl.BlockSpec((1,) + a.shape[1:], lambda b: (b, 0, 0))
    ins = (qa, qb, cka, ka, cva, va, ckb, kb, cvb, vb, za, zb, ga, gb, x, p)
    return pl.pallas_call(
        _sample_attn_kernel,
        grid=(bsz,),
        in_specs=[blk(a) for a in ins]
                 + [_const_spec(biasb.shape), pl.BlockSpec(memory_space=pltpu.SMEM)]
                 + _weight_specs(d, p.shape[-1]),
        out_specs=blk(x),
        out_shape=jax.ShapeDtypeStruct((bsz, seq, d), _F32),
        compiler_params=pltpu.CompilerParams(
            dimension_semantics=("arbitrary",),
            vmem_limit_bytes=VMEM_LIMIT_BYTES),
        name="sample_attn",
    )(*ins, biasb, sink, *weights)


def _dup_cache(c):
    return jnp.concatenate([c[:, :, 0], c[:, :, 0], c[:, :, 1], c[:, :, 1]], axis=-1).astype(_BF16)


def kernel(x_prompt, x_sample, cache_k_win, cache_v_win, cache_k_band, cache_v_band, p_prompt, p_sample, g_in, w_in, g_q_win, g_k_win, sink_win, g_q_band, g_k_band, rel_bias_band, w_o_win, w_o_band, w_out, g_ple, w_ple_gate, w_ple):
    depth = w_in.shape[0]
    bsz, seq, d = x_prompt.shape
    dbsz, dseq, _ = x_sample.shape
    assert seq % PROJ_TILE == 0 and seq % ATTN_TILE == 0 and dseq == CHUNK
    assert cache_k_win.shape[2] == WIN_CHUNKS * CHUNK and cache_k_band.shape[2] == BAND_CHUNKS * CHUNK
    win, band = WIN_CHUNKS * CHUNK, BAND_CHUNKS * CHUNK

    h_p, h_s = x_prompt, x_sample
    outs = [[] for _ in range(8)]
    for i in range(depth):
        pair = lambda g: jnp.tile(g[i], 2)[None, :]
        proj_args = (g_in[i][None, :], w_in[i].astype(_BF16),
                     pair(g_q_win), pair(g_k_win), pair(g_q_band), pair(g_k_band))
        weights = (w_o_win[i].astype(_BF16), w_o_band[i].astype(_BF16), w_out[i].astype(_BF16),
                   g_ple[i][None, :], w_ple_gate[i].astype(_BF16), w_ple[i].astype(_BF16))
        biasb = _band_bias_table(rel_bias_band[i], ATTN_TILE)

        *acts, kat, vat, kbt, vbt = _project(h_p, *proj_args, tile=PROJ_TILE, tail_a=win, tail_b=band)
        h_p = _prompt_attend(acts, h_p, p_prompt[i], biasb, sink_win[i], weights)
        outs[0].append(kat.reshape(bsz, win, WIN_KV_HEADS, HEAD_DIM))
        outs[1].append(vat.reshape(bsz, win, WIN_KV_HEADS, HEAD_DIM))
        outs[2].append(kbt.reshape(bsz, band, BAND_HEADS, HEAD_DIM))
        outs[3].append(vbt.reshape(bsz, band, BAND_HEADS, HEAD_DIM))

        ntok = dbsz * dseq
        *acts, kan, van, kbn, vbn = _project(h_s.reshape(1, ntok, d), *proj_args,
                                             tile=ntok, tail_a=ntok, tail_b=ntok)
        acts = [a.reshape(dbsz, dseq, a.shape[-1]) for a in acts]
        caches = (_dup_cache(cache_k_win[i]), _dup_cache(cache_v_win[i]),
                  cache_k_band[i].reshape(dbsz, band, -1).astype(_BF16),
                  cache_v_band[i].reshape(dbsz, band, -1).astype(_BF16))
        h_s = _sample_attend(acts, caches, h_s, p_sample[i], biasb[:, :dseq], sink_win[i], weights)
        new = lambda t, nh: t.reshape(dbsz, dseq, nh, HEAD_DIM)
        outs[4].append(jnp.concatenate([cache_k_win[i][:, dseq:], new(kan, WIN_KV_HEADS)], axis=1))
        outs[5].append(jnp.concatenate([cache_v_win[i][:, dseq:], new(van, WIN_KV_HEADS)], axis=1))
        outs[6].append(jnp.concatenate([cache_k_band[i][:, dseq:], new(kbn, BAND_HEADS)], axis=1))
        outs[7].append(jnp.concatenate([cache_v_band[i][:, dseq:], new(vbn, BAND_HEADS)], axis=1))
    return (h_p, h_s) + tuple(jnp.stack(o) for o in outs)
```

```python
import functools

import jax
import jax.numpy as jnp
from jax import lax
from jax.experimental import pallas as pl
from jax.experimental.pallas import tpu as pltpu

CHUNK = 64
HEAD_DIM = 64
WIN_HEADS = 8
WIN_KV_HEADS = 2
WIN_CHUNKS = 2
BAND_HEADS = 8
BAND_CHUNKS = 8
REL_CLIP = 128
ALIBI_MAX = 8.0
EPS = 1e-6
NEG_INF = -1e30

LANES = 128
PROJ_TILE = 512
ATTN_TILE = 256
VMEM_LIMIT_BYTES = 56 * 1024 * 1024

_F32 = jnp.float32
_BF16 = jnp.bfloat16
_NT = (((1,), (1,)), ((), ()))


def _rmsnorm(x, g):
    ms = jnp.mean(x * x, axis=-1, keepdims=True)
    return x * lax.rsqrt(ms + EPS) * g


def _lo_lanes(rows=1):
    return lax.broadcasted_iota(jnp.int32, (rows, LANES), 1) < HEAD_DIM


def _head_rmsnorm(u, g_pair):
    lo = _lo_lanes()
    outs = []
    for j in range(u.shape[1] // LANES):
        t = u[:, j * LANES:(j + 1) * LANES]
        sq = t * t
        s_lo = jnp.sum(jnp.where(lo, sq, 0.0), axis=-1, keepdims=True)
        s_hi = jnp.sum(jnp.where(lo, 0.0, sq), axis=-1, keepdims=True)
        ms = jnp.where(lo, s_lo, s_hi) * (1.0 / HEAD_DIM)
        outs.append(t * lax.rsqrt(ms + EPS) * g_pair)
    return outs[0] if len(outs) == 1 else jnp.concatenate(outs, axis=-1)


def _dup_heads(t):
    lo = _lo_lanes()
    r = pltpu.roll(t, HEAD_DIM, axis=1)
    return jnp.concatenate([jnp.where(lo, t, r), jnp.where(lo, r, t)], axis=-1)


_C_QA, _C_KA, _C_VA, _C_ZA, _C_QB, _C_KB, _C_VB, _C_ZB, _C_GA, _C_GB, _C_END = (
    0, 512, 640, 768, 1280, 1792, 2304, 2816, 3328, 4352, 5376)


def _proj_kernel(x_ref, gin_ref, w_ref, gqa_ref, gka_ref, gqb_ref, gkb_ref,
                 qa_ref, ka_ref, va_ref, za_ref, qb_ref, kb_ref, vb_ref, zb_ref, ga_ref, gb_ref,
                 kat_ref, vat_ref, kbt_ref, vbt_ref):
    last = pl.program_id(1) == pl.num_programs(1) - 1
    xn = _rmsnorm(x_ref[0], gin_ref[...]).astype(_BF16)

    def proj(a, b):
        return jnp.dot(xn, w_ref[:, a:b], preferred_element_type=_F32)

    scale = HEAD_DIM ** -0.5
    qa_ref[0] = (_head_rmsnorm(proj(_C_QA, _C_KA), gqa_ref[...]) * scale).astype(_BF16)
    qb_ref[0] = (_head_rmsnorm(proj(_C_QB, _C_KB), gqb_ref[...]) * scale).astype(_BF16)

    ka = _head_rmsnorm(proj(_C_KA, _C_VA), gka_ref[...])
    va = proj(_C_VA, _C_ZA)
    ka_ref[0] = _dup_heads(ka).astype(_BF16)
    va_ref[0] = _dup_heads(va).astype(_BF16)
    kb = _head_rmsnorm(proj(_C_KB, _C_VB), gkb_ref[...])
    vb = proj(_C_VB, _C_ZB)
    kb_ref[0] = kb.astype(_BF16)
    vb_ref[0] = vb.astype(_BF16)

    za_ref[0] = proj(_C_ZA, _C_QB).astype(_BF16)
    zb_ref[0] = proj(_C_ZB, _C_GA).astype(_BF16)
    ga_ref[0] = proj(_C_GA, _C_GB).astype(_BF16)
    gb_ref[0] = proj(_C_GB, _C_END).astype(_BF16)

    @pl.when(last)
    def _():
        ta, tb = kat_ref.shape[1], kbt_ref.shape[1]
        t = ka.shape[0]
        kat_ref[0] = ka[t - ta:, :]
        vat_ref[0] = va[t - ta:, :]
        kbt_ref[0] = kb[t - tb:, :]
        vbt_ref[0] = vb[t - tb:, :]


def _const_spec(shape):
    return pl.BlockSpec(shape, lambda *_: (0,) * len(shape), pipeline_mode=pl.Buffered(1))


def _project(x, gin, w_in, gqa, gka, gqb, gkb, *, tile, tail_a, tail_b):
    bsz, seq, d = x.shape
    nt = seq // tile
    row = lambda w: pl.BlockSpec((1, tile, w), lambda b, t: (b, t, 0))
    tail = lambda r, w: pl.BlockSpec((1, r, w), lambda b, t: (b, 0, 0))
    act = lambda w: jax.ShapeDtypeStruct((bsz, seq, w), _BF16)
    widths = (512, 256, 256, 512, 512, 512, 512, 512, 1024, 1024)
    return pl.pallas_call(
        _proj_kernel,
        grid=(bsz, nt),
        in_specs=[row(d), _const_spec((1, d)), _const_spec(w_in.shape),
                  _const_spec((1, LANES)), _const_spec((1, LANES)),
                  _const_spec((1, LANES)), _const_spec((1, LANES))],
        out_specs=[row(w) for w in widths] + [tail(tail_a, 128), tail(tail_a, 128),
                                              tail(tail_b, 512), tail(tail_b, 512)],
        out_shape=[act(w) for w in widths] + [
            jax.ShapeDtypeStruct((bsz, tail_a, 128), _F32),
            jax.ShapeDtypeStruct((bsz, tail_a, 128), _F32),
            jax.ShapeDtypeStruct((bsz, tail_b, 512), _F32),
            jax.ShapeDtypeStruct((bsz, tail_b, 512), _F32)],
        compiler_params=pltpu.CompilerParams(
            dimension_semantics=("arbitrary", "arbitrary"),
            vmem_limit_bytes=VMEM_LIMIT_BYTES),
        name="proj",
    )(x, gin, w_in, gqa, gka, gqb, gkb)


def _mixer(q, k_pieces, v_pieces, bias_fn, sink_fn, n_heads, kv_col):
    lo = _lo_lanes()
    outs = []
    for p in range(n_heads // 2):
        qp = q[:, p * LANES:(p + 1) * LANES]
        halves = []
        for half in range(2):
            h = 2 * p + half
            qm = jnp.where(lo if half == 0 else jnp.logical_not(lo), qp, jnp.zeros_like(qp))
            c0 = kv_col(h)
            s = [lax.dot_general(qm, kp[:, c0:c0 + LANES], _NT, preferred_element_type=_F32)
                 + bias_fn(h, j) for j, kp in enumerate(k_pieces)]
            m = functools.reduce(jnp.maximum, [jnp.max(sj, axis=-1, keepdims=True) for sj in s])
            if sink_fn is not None:
                sk = sink_fn(h)
                m = jnp.maximum(m, sk)
            e = [jnp.exp(sj - m) for sj in s]
            den = functools.reduce(jnp.add, [jnp.sum(ej, axis=-1, keepdims=True) for ej in e])
            if sink_fn is not None:
                den = den + jnp.exp(sk - m)
            o = functools.reduce(jnp.add, [
                jnp.dot(ej.astype(_BF16), vp[:, c0:c0 + LANES], preferred_element_type=_F32)
                for ej, vp in zip(e, v_pieces)])
            halves.append(o / den)
        outs.append(jnp.where(lo, halves[0], halves[1]))
    return jnp.concatenate(outs, axis=-1)


def _alibi_pieces(rows, piece_rows, chunk_mask):
    out, c_off = [], 0
    for n in piece_rows:
        r = lax.broadcasted_iota(jnp.int32, (rows, n), 0)
        c = lax.broadcasted_iota(jnp.int32, (rows, n), 1) + c_off
        dist = jnp.abs(r + WIN_CHUNKS * CHUNK - c).astype(_F32)
        valid = None
        if chunk_mask:
            qc, kc = r // CHUNK, c // CHUNK
            valid = jnp.logical_and(kc >= qc, kc <= qc + WIN_CHUNKS)
        out.append((dist, valid))
        c_off += n
    return out


def _layer_body(qa, qb, ka_p, va_p, kb_p, vb_p, alibi, a_pen, bias_b_fn, sink_ref,
                za, zb, ga, gb, x, p, woa_ref, wob_ref, wout_ref, gple_ref, wpg_ref, wple_ref):
    def bias_a(h, j):
        dist, valid = alibi[j]
        slope = 2.0 ** (-ALIBI_MAX * (h + 1) / WIN_HEADS)
        b = dist * (-slope)
        if valid is not None:
            b = jnp.where(valid, b, NEG_INF)
        if a_pen[j] is not None:
            b = b + a_pen[j]
        return b

    oa = _mixer(qa, ka_p, va_p, bias_a, lambda h: sink_ref[h], WIN_HEADS,
                lambda h: LANES * (h // (WIN_HEADS // WIN_KV_HEADS)))
    ob = _mixer(qb, kb_p, vb_p, bias_b_fn, None, BAND_HEADS, lambda h: LANES * (h // 2))

    za = za.astype(_F32)
    zb = zb.astype(_F32)
    br_a = jnp.dot((oa * jax.nn.silu(za)).astype(_BF16), woa_ref[...], preferred_element_type=_F32)
    br_b = jnp.dot((ob * jax.nn.silu(zb)).astype(_BF16), wob_ref[...], preferred_element_type=_F32)
    mrg = jax.nn.sigmoid(ga.astype(_F32)) * br_a + jax.nn.sigmoid(gb.astype(_F32)) * br_b
    h2 = x + jnp.dot(mrg.astype(_BF16), wout_ref[...], preferred_element_type=_F32)
    gate = jax.nn.sigmoid(jnp.dot(_rmsnorm(h2, gple_ref[...]).astype(_BF16), wpg_ref[...],
                                  preferred_element_type=_F32))
    return h2 + gate * jnp.dot(p.astype(_BF16), wple_ref[...], preferred_element_type=_F32)


def _prompt_attn_kernel(qa_ref, qb_ref, kah_ref, kac_ref, vah_ref, vac_ref,
                        kb2_ref, kb1_ref, kb0_ref, vb2_ref, vb1_ref, vb0_ref,
                        za_ref, zb_ref, ga_ref, gb_ref, x_ref, p_ref, biasb_ref, sink_ref,
                        woa_ref, wob_ref, wout_ref, gple_ref, wpg_ref, wple_ref, o_ref):
    i = pl.program_id(1)
    rows = qa_ref.shape[1]
    pen = lambda need: jnp.where(i >= need, 0.0, NEG_INF).astype(_F32)
    alibi = _alibi_pieces(rows, (kah_ref.shape[1], kac_ref.shape[1]), True)
    b_pen = (pen(2), pen(1), None)

    def bias_b(h, j):
        b = biasb_ref[h, :, j * rows:(j + 1) * rows]
        return b if b_pen[j] is None else b + b_pen[j]

    o_ref[0] = _layer_body(
        qa_ref[0], qb_ref[0], (kah_ref[0], kac_ref[0]), (vah_ref[0], vac_ref[0]),
        (kb2_ref[0], kb1_ref[0], kb0_ref[0]), (vb2_ref[0], vb1_ref[0], vb0_ref[0]),
        alibi, (pen(1), None), bias_b, sink_ref,
        za_ref[0], zb_ref[0], ga_ref[0], gb_ref[0], x_ref[0], p_ref[0],
        woa_ref, wob_ref, wout_ref, gple_ref, wpg_ref, wple_ref)


def _sample_attn_kernel(qa_ref, qb_ref, kac_ref, kan_ref, vac_ref, van_ref,
                        kbc_ref, kbn_ref, vbc_ref, vbn_ref,
                        za_ref, zb_ref, ga_ref, gb_ref, x_ref, p_ref, biasb_ref, sink_ref,
                        woa_ref, wob_ref, wout_ref, gple_ref, wpg_ref, wple_ref, o_ref):
    rows = qa_ref.shape[1]
    half = kbc_ref.shape[1] // 2
    alibi = _alibi_pieces(rows, (kac_ref.shape[1], kan_ref.shape[1]), False)
    edges = (0, half, 2 * half, 2 * half + rows)

    def bias_b(h, j):
        return biasb_ref[h, :, edges[j]:edges[j + 1]]

    kbc, vbc = kbc_ref[0], vbc_ref[0]
    o_ref[0] = _layer_body(
        qa_ref[0], qb_ref[0], (kac_ref[0], kan_ref[0]), (vac_ref[0], van_ref[0]),
        (kbc[:half], kbc[half:], kbn_ref[0]), (vbc[:half], vbc[half:], vbn_ref[0]),
        alibi, (None, None), bias_b, sink_ref,
        za_ref[0], zb_ref[0], ga_ref[0], gb_ref[0], x_ref[0], p_ref[0],
        woa_ref, wob_ref, wout_ref, gple_ref, wpg_ref, wple_ref)


def _band_bias_kernel(w_ref, o_ref):
    rows, width = o_ref.shape[1:]
    t = pltpu.roll(jnp.broadcast_to(w_ref[0], (rows, width)), 0, 1, stride=1, stride_axis=0)
    qc = lax.broadcasted_iota(jnp.int32, (rows, width), 0) // CHUNK
    kc = lax.broadcasted_iota(jnp.int32, (rows, width), 1) // CHUNK
    valid = jnp.logical_and(kc >= qc, kc <= qc + BAND_CHUNKS)
    o_ref[0] = jnp.where(valid, t, NEG_INF)


def _band_bias_table(table, rows):
    heads = table.shape[0]
    band = BAND_CHUNKS * CHUNK
    width = rows + band
    assert rows >= 2 * CHUNK and rows % CHUNK == 0
    far = table[:, 2 * REL_CLIP:]
    w = jnp.concatenate([jnp.broadcast_to(far, (heads, band - REL_CLIP)), table[:, ::-1],
                         jnp.broadcast_to(far, (heads, width - band - REL_CLIP - 1))], axis=1)
    return pl.pallas_call(
        _band_bias_kernel,
        grid=(heads,),
        in_specs=[pl.BlockSpec((1, 1, width), lambda h: (h, 0, 0))],
        out_specs=pl.BlockSpec((1, rows, width), lambda h: (h, 0, 0)),
        out_shape=jax.ShapeDtypeStruct((heads, rows, width), _F32),
        name="band_bias",
    )(w.astype(_F32)[:, None, :])


def _weight_specs(d, ple):
    return [_const_spec((512, d)), _const_spec((512, d)), _const_spec((d, d)),
            _const_spec((1, d)), _const_spec((d, d)), _const_spec((ple, d))]


def _prompt_attend(acts, x, p, biasb, sink, weights):
    qa, ka, va, za, qb, kb, vb, zb, ga, gb = acts
    bsz, seq, d = x.shape
    tq = ATTN_TILE
    cur = lambda w: pl.BlockSpec((1, tq, w), lambda b, i: (b, i, 0))
    prev = lambda w, n: pl.BlockSpec((1, tq, w), lambda b, i: (b, jnp.maximum(i - n, 0), 0))
    halo_rows = WIN_CHUNKS * CHUNK
    halo = pl.BlockSpec((1, halo_rows, 256),
                        lambda b, i: (b, jnp.maximum(i * (tq // halo_rows) - 1, 0), 0))
    return pl.pallas_call(
        _prompt_attn_kernel,
        grid=(bsz, seq // tq),
        in_specs=[cur(512), cur(512), halo, cur(256), halo, cur(256),
                  prev(512, 2), prev(512, 1), cur(512), prev(512, 2), prev(512, 1), cur(512),
                  cur(512), cur(512), cur(d), cur(d), cur(d), cur(p.shape[-1]),
                  _const_spec(biasb.shape), pl.BlockSpec(memory_space=pltpu.SMEM)]
                 + _weight_specs(d, p.shape[-1]),
        out_specs=cur(d),
        out_shape=jax.ShapeDtypeStruct((bsz, seq, d), _F32),
        compiler_params=pltpu.CompilerParams(
            dimension_semantics=("arbitrary", "arbitrary"),
            vmem_limit_bytes=VMEM_LIMIT_BYTES),
        name="prompt_attn",
    )(qa, qb, ka, ka, va, va, kb, kb, kb, vb, vb, vb, za, zb, ga, gb, x, p, biasb, sink, *weights)


def _sample_attend(acts, caches, x, p, biasb, sink, weights):
    qa, ka, va, za, qb, kb, vb, zb, ga, gb = acts
    cka, cva, ckb, cvb = caches
    bsz, seq, d = x.shape
    blk = lambda a: pl.BlockSpec((1,) + a.shape[1:], lambda b: (b, 0, 0))
    ins = (qa, qb, cka, ka, cva, va, ckb, kb, cvb, vb, za, zb, ga, gb, x, p)
    return pl.pallas_call(
        _sample_attn_kernel,
        grid=(bsz,),
        in_specs=[blk(a) for a in ins]
                 + [_const_spec(biasb.shape), pl.BlockSpec(memory_space=pltpu.SMEM)]
                 + _weight_specs(d, p.shape[-1]),
        out_specs=blk(x),
        out_shape=jax.ShapeDtypeStruct((bsz, seq, d), _F32),
        compiler_params=pltpu.CompilerParams(
            dimension_semantics=("arbitrary",),
            vmem_limit_bytes=VMEM_LIMIT_BYTES),
        name="sample_attn",
    )(*ins, biasb, sink, *weights)


def _dup_cache(c):
    return jnp.concatenate([c[:, :, 0], c[:, :, 0], c[:, :, 1], c[:, :, 1]], axis=-1).astype(_BF16)


def kernel(x_prompt, x_sample, cache_k_win, cache_v_win, cache_k_band, cache_v_band, p_prompt, p_sample, g_in, w_in, g_q_win, g_k_win, sink_win, g_q_band, g_k_band, rel_bias_band, w_o_win, w_o_band, w_out, g_ple, w_ple_gate, w_ple):
    depth = w_in.shape[0]
    bsz, seq, d = x_prompt.shape
    dbsz, dseq, _ = x_sample.shape
    assert seq % PROJ_TILE == 0 and seq % ATTN_TILE == 0 and dseq == CHUNK
    assert cache_k_win.shape[2] == WIN_CHUNKS * CHUNK and cache_k_band.shape[2] == BAND_CHUNKS * CHUNK
    win, band = WIN_CHUNKS * CHUNK, BAND_CHUNKS * CHUNK

    h_p, h_s = x_prompt, x_sample
    outs = [[] for _ in range(8)]
    for i in range(depth):
        pair = lambda g: jnp.tile(g[i], 2)[None, :]
        proj_args = (g_in[i][None, :], w_in[i].astype(_BF16),
                     pair(g_q_win), pair(g_k_win), pair(g_q_band), pair(g_k_band))
        weights = (w_o_win[i].astype(_BF16), w_o_band[i].astype(_BF16), w_out[i].astype(_BF16),
                   g_ple[i][None, :], w_ple_gate[i].astype(_BF16), w_ple[i].astype(_BF16))
        biasb = _band_bias_table(rel_bias_band[i], ATTN_TILE)

        *acts, kat, vat, kbt, vbt = _project(h_p, *proj_args, tile=PROJ_TILE, tail_a=win, tail_b=band)
        h_p = _prompt_attend(acts, h_p, p_prompt[i], biasb, sink_win[i], weights)
        outs[0].append(kat.reshape(bsz, win, WIN_KV_HEADS, HEAD_DIM))
        outs[1].append(vat.reshape(bsz, win, WIN_KV_HEADS, HEAD_DIM))
        outs[2].append(kbt.reshape(bsz, band, BAND_HEADS, HEAD_DIM))
        outs[3].append(vbt.reshape(bsz, band, BAND_HEADS, HEAD_DIM))

        ntok = dbsz * dseq
        *acts, kan, van, kbn, vbn = _project(h_s.reshape(1, ntok, d), *proj_args,
                                             tile=ntok, tail_a=ntok, tail_b=ntok)
        acts = [a.reshape(dbsz, dseq, a.shape[-1]) for a in acts]
        caches = (_dup_cache(cache_k_win[i]), _dup_cache(cache_v_win[i]),
                  cache_k_band[i].reshape(dbsz, band, -1).astype(_BF16),
                  cache_v_band[i].reshape(dbsz, band, -1).astype(_BF16))
        h_s = _sample_attend(acts, caches, h_s, p_sample[i], biasb[:, :dseq], sink_win[i], weights)
        new = lambda t, nh: t.reshape(dbsz, dseq, nh, HEAD_DIM)
        outs[4].append(jnp.concatenate([cache_k_win[i][:, dseq:], new(kan, WIN_KV_HEADS)], axis=1))
        outs[5].append(jnp.concatenate([cache_v_win[i][:, dseq:], new(van, WIN_KV_HEADS)], axis=1))
        outs[6].append(jnp.concatenate([cache_k_band[i][:, dseq:], new(kbn, BAND_HEADS)], axis=1))
        outs[7].append(jnp.concatenate([cache_v_band[i][:, dseq:], new(vbn, BAND_HEADS)], axis=1))
    return (h_p, h_s) + tuple(jnp.stack(o) for o in outs)
```

```python
import functools
import math

import jax
import jax.numpy as jnp
from jax import lax
from jax.experimental import pallas as pl
from jax.experimental.pallas import tpu as pltpu

CHUNK = 64
HEAD_DIM = 64
WIN_HEADS = 8
WIN_KV_HEADS = 2
WIN_CHUNKS = 2
BAND_HEADS = 8
BAND_CHUNKS = 8
REL_CLIP = 128
ALIBI_MAX = 8.0
EPS = 1e-6
NEG_INF = -1e30
LOG2E = math.log2(math.e)

WIN = WIN_CHUNKS * CHUNK
BAND = BAND_CHUNKS * CHUNK
LANES = 128
PROJ_TILE = 512
ATTN_TILE = 256
WIN_BLOCK = 128
QK_LOOKAHEAD = 3
VMEM_LIMIT_BYTES = 56 * 1024 * 1024

_F32 = jnp.float32
_BF16 = jnp.bfloat16
_NT = (((1,), (1,)), ((), ()))


def _rmsnorm(x, g):
    ms = jnp.mean(x * x, axis=-1, keepdims=True)
    return x * lax.rsqrt(ms + EPS) * g


def _lo_lanes(rows=1):
    return lax.broadcasted_iota(jnp.int32, (rows, LANES), 1) < HEAD_DIM


def _head_rmsnorm(u, g_pair):
    lo = _lo_lanes()
    outs = []
    for j in range(u.shape[1] // LANES):
        t = u[:, j * LANES:(j + 1) * LANES]
        sq = t * t
        s_lo = jnp.sum(jnp.where(lo, sq, 0.0), axis=-1, keepdims=True)
        s_hi = jnp.sum(jnp.where(lo, 0.0, sq), axis=-1, keepdims=True)
        ms = jnp.where(lo, s_lo, s_hi) * (1.0 / HEAD_DIM)
        outs.append(t * lax.rsqrt(ms + EPS) * g_pair)
    return outs[0] if len(outs) == 1 else jnp.concatenate(outs, axis=-1)


def _dup_heads(t):
    lo = _lo_lanes()
    r = pltpu.roll(t, HEAD_DIM, axis=1)
    return jnp.concatenate([jnp.where(lo, t, r), jnp.where(lo, r, t)], axis=-1)


_C_QA, _C_KA, _C_VA, _C_ZA, _C_QB, _C_KB, _C_VB, _C_ZB, _C_GA, _C_GB, _C_END = (
    0, 512, 640, 768, 1280, 1792, 2304, 2816, 3328, 4352, 5376)


def _proj_kernel(x_ref, gin_ref, w_ref, gqa_ref, gka_ref, gqb_ref, gkb_ref,
                 qa_ref, ka_ref, va_ref, za_ref, qb_ref, kb_ref, vb_ref, zb_ref, ga_ref, gb_ref,
                 kat_ref, vat_ref, kbt_ref, vbt_ref):
    last = pl.program_id(1) == pl.num_programs(1) - 1
    xn = _rmsnorm(x_ref[0], gin_ref[...]).astype(_BF16)

    def proj(a, b):
        return jnp.dot(xn, w_ref[:, a:b], preferred_element_type=_F32)

    scale = HEAD_DIM ** -0.5 * LOG2E
    qa_ref[0] = (_head_rmsnorm(proj(_C_QA, _C_KA), gqa_ref[...]) * scale).astype(_BF16)
    qb_ref[0] = (_head_rmsnorm(proj(_C_QB, _C_KB), gqb_ref[...]) * scale).astype(_BF16)

    ka = _head_rmsnorm(proj(_C_KA, _C_VA), gka_ref[...])
    va = proj(_C_VA, _C_ZA)
    ka_ref[0] = _dup_heads(ka).astype(_BF16)
    va_ref[0] = _dup_heads(va).astype(_BF16)
    kb = _head_rmsnorm(proj(_C_KB, _C_VB), gkb_ref[...])
    vb = proj(_C_VB, _C_ZB)
    kb_ref[0] = kb.astype(_BF16)
    vb_ref[0] = vb.astype(_BF16)

    za_ref[0] = proj(_C_ZA, _C_QB).astype(_BF16)
    zb_ref[0] = proj(_C_ZB, _C_GA).astype(_BF16)
    ga_ref[0] = proj(_C_GA, _C_GB).astype(_BF16)
    gb_ref[0] = proj(_C_GB, _C_END).astype(_BF16)

    @pl.when(last)
    def _():
        ta, tb = kat_ref.shape[1], kbt_ref.shape[1]
        t = ka.shape[0]
        kat_ref[0] = ka[t - ta:, :]
        vat_ref[0] = va[t - ta:, :]
        kbt_ref[0] = kb[t - tb:, :]
        vbt_ref[0] = vb[t - tb:, :]


def _const_spec(shape, index=None):
    index = (0,) * len(shape) if index is None else index
    return pl.BlockSpec(shape, lambda *_: index, pipeline_mode=pl.Buffered(1))


def _project(x, gin, w_in, gqa, gka, gqb, gkb, *, tile, tail_a, tail_b):
    bsz, seq, d = x.shape
    nt = seq // tile
    row = lambda w: pl.BlockSpec((1, tile, w), lambda b, t: (b, t, 0))
    tail = lambda r, w: pl.BlockSpec((1, r, w), lambda b, t: (b, 0, 0))
    act = lambda w: jax.ShapeDtypeStruct((bsz, seq, w), _BF16)
    widths = (512, 256, 256, 512, 512, 512, 512, 512, 1024, 1024)
    return pl.pallas_call(
        _proj_kernel,
        grid=(bsz, nt),
        in_specs=[row(d), _const_spec((1, d)), _const_spec(w_in.shape),
                  _const_spec((1, LANES)), _const_spec((1, LANES)),
                  _const_spec((1, LANES)), _const_spec((1, LANES))],
        out_specs=[row(w) for w in widths] + [tail(tail_a, 128), tail(tail_a, 128),
                                              tail(tail_b, 512), tail(tail_b, 512)],
        out_shape=[act(w) for w in widths] + [
            jax.ShapeDtypeStruct((bsz, tail_a, 128), _F32),
            jax.ShapeDtypeStruct((bsz, tail_a, 128), _F32),
            jax.ShapeDtypeStruct((bsz, tail_b, 512), _F32),
            jax.ShapeDtypeStruct((bsz, tail_b, 512), _F32)],
        compiler_params=pltpu.CompilerParams(
            dimension_semantics=("arbitrary", "arbitrary"),
            vmem_limit_bytes=VMEM_LIMIT_BYTES),
        name="proj",
    )(x, gin, w_in, gqa, gka, gqb, gkb)


def _win_bias_kernel(o_ref):
    heads, rows, width = o_ref.shape[1:]
    r = lax.broadcasted_iota(jnp.int32, (rows, width), 0)
    c = lax.broadcasted_iota(jnp.int32, (rows, width), 1)
    dist = jnp.abs(r + WIN - c).astype(_F32)
    qc, kc = r // CHUNK, c // CHUNK
    valid = jnp.logical_and(kc >= qc, kc <= qc + WIN_CHUNKS)
    valid = jnp.logical_and(valid, c >= (1 - pl.program_id(0)) * WIN)
    for h in range(heads):
        slope = 2.0 ** (-ALIBI_MAX * (h + 1) / heads)
        o_ref[0, h] = jnp.where(valid, dist * (-slope * LOG2E), NEG_INF)


def _win_bias_table(heads):
    shape = (2, heads, WIN_BLOCK, WIN_BLOCK + WIN)
    return pl.pallas_call(
        _win_bias_kernel,
        grid=(2,),
        out_specs=pl.BlockSpec((1,) + shape[1:], lambda v: (v, 0, 0, 0)),
        out_shape=jax.ShapeDtypeStruct(shape, _F32),
        name="win_bias",
    )()


def _band_bias_kernel(w_ref, o_ref):
    rows, width = o_ref.shape[2:]
    t = pltpu.roll(jnp.broadcast_to(w_ref[0], (rows, width)), 0, 1, stride=1, stride_axis=0)
    c = lax.broadcasted_iota(jnp.int32, (rows, width), 1)
    qc = lax.broadcasted_iota(jnp.int32, (rows, width), 0) // CHUNK
    kc = c // CHUNK
    valid = jnp.logical_and(kc >= qc, kc <= qc + BAND_CHUNKS)
    valid = jnp.logical_and(valid, c >= (BAND // rows - pl.program_id(0)) * rows)
    o_ref[0, 0] = jnp.where(valid, t * LOG2E, NEG_INF)


def _band_bias_table(table, rows):
    heads = table.shape[0]
    width = rows + BAND
    variants = BAND // rows + 1
    assert rows >= 2 * CHUNK and rows % CHUNK == 0 and BAND % rows == 0
    far = table[:, 2 * REL_CLIP:]
    w = jnp.concatenate([jnp.broadcast_to(far, (heads, BAND - REL_CLIP)), table[:, ::-1],
                         jnp.broadcast_to(far, (heads, width - BAND - REL_CLIP - 1))], axis=1)
    return pl.pallas_call(
        _band_bias_kernel,
        grid=(variants, heads),
        in_specs=[pl.BlockSpec((1, 1, width), lambda v, h: (h, 0, 0))],
        out_specs=pl.BlockSpec((1, 1, rows, width), lambda v, h: (v, h, 0, 0)),
        out_shape=jax.ShapeDtypeStruct((variants, heads, rows, width), _F32),
        name="band_bias",
    )(w.astype(_F32)[:, None, :])


def _reduce_rows(pieces, elementwise, lane_reduce):
    by_width = {}
    for x in pieces:
        n = x.shape[1]
        by_width[n] = x if n not in by_width else elementwise(by_width[n], x)
    return functools.reduce(elementwise, [lane_reduce(x, axis=-1, keepdims=True)
                                          for x in by_width.values()])


def _attention(tasks, lookahead):
    def scores(task):
        q, ks, _, bs, _ = task
        return [lax.dot_general(q, k, _NT, preferred_element_type=_F32) + b()
                for k, b in zip(ks, bs)]

    pending = [scores(t) for t in tasks[:lookahead]]
    outs = []
    for idx, (_, _, vs, _, sink) in enumerate(tasks):
        if idx + lookahead < len(tasks):
            pending.append(scores(tasks[idx + lookahead]))
        s = pending[idx]
        pending[idx] = None
        m = _reduce_rows(s, jnp.maximum, jnp.max)
        if sink is not None:
            m = jnp.maximum(m, sink)
        e = [jnp.exp2(sj - m) for sj in s]
        den = _reduce_rows(e, jnp.add, jnp.sum)
        if sink is not None:
            den = den + jnp.exp2(sink - m)
        o = functools.reduce(jnp.add, [
            jnp.dot(ej.astype(_BF16), v, preferred_element_type=_F32) for ej, v in zip(e, vs)])
        outs.append(o / den)
    return outs


def _masked_q(q, h):
    qp = q[:, (h // 2) * LANES:(h // 2 + 1) * LANES]
    keep = _lo_lanes() if h % 2 == 0 else jnp.logical_not(_lo_lanes())
    return jnp.where(keep, qp, jnp.zeros_like(qp))


def _pair_tiles(per_head):
    lo = _lo_lanes()
    return jnp.concatenate([jnp.where(lo, per_head[2 * p], per_head[2 * p + 1])
                            for p in range(len(per_head) // 2)], axis=-1)


def _finish(oa, ob, za, zb, ga, gb, x, p, woa_ref, wob_ref, wout_ref, gple_ref, wpg_ref, wple_ref):
    za = za.astype(_F32)
    zb = zb.astype(_F32)
    br_a = jnp.dot((oa * jax.nn.silu(za)).astype(_BF16), woa_ref[...], preferred_element_type=_F32)
    br_b = jnp.dot((ob * jax.nn.silu(zb)).astype(_BF16), wob_ref[...], preferred_element_type=_F32)
    mrg = jax.nn.sigmoid(ga.astype(_F32)) * br_a + jax.nn.sigmoid(gb.astype(_F32)) * br_b
    h2 = x + jnp.dot(mrg.astype(_BF16), wout_ref[...], preferred_element_type=_F32)
    gate = jax.nn.sigmoid(jnp.dot(_rmsnorm(h2, gple_ref[...]).astype(_BF16), wpg_ref[...],
                                  preferred_element_type=_F32))
    return h2 + gate * jnp.dot(p.astype(_BF16), wple_ref[...], preferred_element_type=_F32)


def _win_col(h):
    return LANES * (h // (WIN_HEADS // WIN_KV_HEADS))


def _band_col(h):
    return LANES * (h // 2)


def _prompt_attn_kernel(qa_ref, qb_ref, kah_ref, kac_ref, vah_ref, vac_ref,
                        kb2_ref, kb1_ref, kb0_ref, vb2_ref, vb1_ref, vb0_ref,
                        za_ref, zb_ref, ga_ref, gb_ref, x_ref, p_ref,
                        ta0_ref, ta1_ref, tb_ref, sink_ref,
                        woa_ref, wob_ref, wout_ref, gple_ref, wpg_ref, wple_ref, o_ref):
    rows = qa_ref.shape[1]
    n_blk = rows // WIN_BLOCK
    qa, qb = qa_ref[0], qb_ref[0]
    ka = jnp.concatenate([kah_ref[0], kac_ref[0]], axis=0)
    va = jnp.concatenate([vah_ref[0], vac_ref[0]], axis=0)
    ka_blk = [ka[j * WIN_BLOCK:(j + 1) * WIN_BLOCK + WIN] for j in range(n_blk)]
    va_blk = [va[j * WIN_BLOCK:(j + 1) * WIN_BLOCK + WIN] for j in range(n_blk)]
    kb = (kb2_ref[0], kb1_ref[0], kb0_ref[0])
    vb = (vb2_ref[0], vb1_ref[0], vb0_ref[0])

    tasks = []
    for h in range(BAND_HEADS):
        c = _band_col(h)
        tasks.append((_masked_q(qb, h), [k[:, c:c + LANES] for k in kb], [v[:, c:c + LANES] for v in vb],
                      [functools.partial(lambda h, j: tb_ref[0, h, :, j * rows:(j + 1) * rows], h, j)
                       for j in range(len(kb))], None))
        c = _win_col(h)
        qm = _masked_q(qa, h)
        for j in range(n_blk):
            tref = ta0_ref if j == 0 else ta1_ref
            tasks.append((qm[j * WIN_BLOCK:(j + 1) * WIN_BLOCK], [ka_blk[j][:, c:c + LANES]],
                          [va_blk[j][:, c:c + LANES]], [functools.partial(lambda t, h: t[0, h], tref, h)],
                          sink_ref[h] * LOG2E))
    outs = _attention(tasks, QK_LOOKAHEAD)
    per = 1 + n_blk
    ob = _pair_tiles([outs[h * per] for h in range(BAND_HEADS)])
    oa = _pair_tiles([jnp.concatenate(outs[h * per + 1:(h + 1) * per], axis=0) for h in range(WIN_HEADS)])
    o_ref[0] = _finish(oa, ob, za_ref[0], zb_ref[0], ga_ref[0], gb_ref[0], x_ref[0], p_ref[0],
                       woa_ref, wob_ref, wout_ref, gple_ref, wpg_ref, wple_ref)


def _sample_attn_kernel(qa_ref, qb_ref, kac_ref, kan_ref, vac_ref, van_ref,
                        kbc_ref, kbn_ref, vbc_ref, vbn_ref,
                        za_ref, zb_ref, ga_ref, gb_ref, x_ref, p_ref, ta_ref, tb_ref, sink_ref,
                        woa_ref, wob_ref, wout_ref, gple_ref, wpg_ref, wple_ref, o_ref):
    rows = qa_ref.shape[1]
    half = kbc_ref.shape[1] // 2
    qa, qb = qa_ref[0], qb_ref[0]
    kbc, vbc = kbc_ref[0], vbc_ref[0]
    ka, va = (kac_ref[0], kan_ref[0]), (vac_ref[0], van_ref[0])
    kb, vb = (kbc[:half], kbc[half:], kbn_ref[0]), (vbc[:half], vbc[half:], vbn_ref[0])
    a_edges = (0, ka[0].shape[0], ka[0].shape[0] + rows)
    b_edges = (0, half, 2 * half, 2 * half + rows)

    tasks = []
    for h in range(BAND_HEADS):
        c = _band_col(h)
        tasks.append((_masked_q(qb, h), [k[:, c:c + LANES] for k in kb], [v[:, c:c + LANES] for v in vb],
                      [functools.partial(lambda h, j: tb_ref[0, h, :, b_edges[j]:b_edges[j + 1]], h, j)
                       for j in range(len(kb))], None))
        c = _win_col(h)
        tasks.append((_masked_q(qa, h), [k[:, c:c + LANES] for k in ka], [v[:, c:c + LANES] for v in va],
                      [functools.partial(lambda h, j: ta_ref[0, h, :, a_edges[j]:a_edges[j + 1]], h, j)
                       for j in range(len(ka))], sink_ref[h] * LOG2E))
    outs = _attention(tasks, QK_LOOKAHEAD)
    ob = _pair_tiles(outs[0::2])
    oa = _pair_tiles(outs[1::2])
    o_ref[0] = _finish(oa, ob, za_ref[0], zb_ref[0], ga_ref[0], gb_ref[0], x_ref[0], p_ref[0],
                       woa_ref, wob_ref, wout_ref, gple_ref, wpg_ref, wple_ref)


def _weight_specs(d, ple):
    return [_const_spec((512, d)), _const_spec((512, d)), _const_spec((d, d)),
            _const_spec((1, d)), _const_spec((d, d)), _const_spec((ple, d))]


def _prompt_attend(acts, x, p, ta, tb, sink, weights):
    qa, ka, va, za, qb, kb, vb, zb, ga, gb = acts
    bsz, seq, d = x.shape
    tq = ATTN_TILE
    cur = lambda w: pl.BlockSpec((1, tq, w), lambda b, i: (b, i, 0))
    prev = lambda w, n: pl.BlockSpec((1, tq, w), lambda b, i: (b, jnp.maximum(i - n, 0), 0))
    halo = pl.BlockSpec((1, WIN, 256), lambda b, i: (b, jnp.maximum(i * (tq // WIN) - 1, 0), 0))
    ta_first = pl.BlockSpec((1,) + ta.shape[1:], lambda b, i: (jnp.minimum(i, 1), 0, 0, 0))
    tb_var = pl.BlockSpec((1,) + tb.shape[1:], lambda b, i: (jnp.minimum(i, tb.shape[0] - 1), 0, 0, 0))
    return pl.pallas_call(
        _prompt_attn_kernel,
        grid=(bsz, seq // tq),
        in_specs=[cur(512), cur(512), halo, cur(256), halo, cur(256),
                  prev(512, 2), prev(512, 1), cur(512), prev(512, 2), prev(512, 1), cur(512),
                  cur(512), cur(512), cur(d), cur(d), cur(d), cur(p.shape[-1]),
                  ta_first, _const_spec((1,) + ta.shape[1:], (1, 0, 0, 0)), tb_var,
                  pl.BlockSpec(memory_space=pltpu.SMEM)]
                 + _weight_specs(d, p.shape[-1]),
        out_specs=cur(d),
        out_shape=jax.ShapeDtypeStruct((bsz, seq, d), _F32),
        compiler_params=pltpu.CompilerParams(
            dimension_semantics=("arbitrary", "arbitrary"),
            vmem_limit_bytes=VMEM_LIMIT_BYTES),
        name="prompt_attn",
    )(qa, qb, ka, ka, va, va, kb, kb, kb, vb, vb, vb, za, zb, ga, gb, x, p, ta, ta, tb, sink, *weights)


def _sample_attend(acts, caches, x, p, ta, tb, sink, weights):
    qa, ka, va, za, qb, kb, vb, zb, ga, gb = acts
    cka, cva, ckb, cvb = caches
    bsz, seq, d = x.shape
    blk = lambda a: pl.BlockSpec((1,) + a.shape[1:], lambda b: (b, 0, 0))
    ins = (qa, qb, cka, ka, cva, va, ckb, kb, cvb, vb, za, zb, ga, gb, x, p)
    ta_spec = _const_spec((1, ta.shape[1], seq, ta.shape[3]), (ta.shape[0] - 1, 0, 0, 0))
    tb_spec = _const_spec((1, tb.shape[1], seq, tb.shape[3]), (tb.shape[0] - 1, 0, 0, 0))
    return pl.pallas_call(
        _sample_attn_kernel,
        grid=(bsz,),
        in_specs=[blk(a) for a in ins] + [ta_spec, tb_spec, pl.BlockSpec(memory_space=pltpu.SMEM)]
                 + _weight_specs(d, p.shape[-1]),
        out_specs=blk(x),
        out_shape=jax.ShapeDtypeStruct((bsz, seq, d), _F32),
        compiler_params=pltpu.CompilerParams(
            dimension_semantics=("arbitrary",),
            vmem_limit_bytes=VMEM_LIMIT_BYTES),
        name="sample_attn",
    )(*ins, ta, tb, sink, *weights)


def _dup_cache(c):
    return jnp.concatenate([c[:, :, 0], c[:, :, 0], c[:, :, 1], c[:, :, 1]], axis=-1).astype(_BF16)


def kernel(x_prompt, x_sample, cache_k_win, cache_v_win, cache_k_band, cache_v_band, p_prompt, p_sample, g_in, w_in, g_q_win, g_k_win, sink_win, g_q_band, g_k_band, rel_bias_band, w_o_win, w_o_band, w_out, g_ple, w_ple_gate, w_ple):
    depth = w_in.shape[0]
    bsz, seq, d = x_prompt.shape
    dbsz, dseq, _ = x_sample.shape
    assert seq % PROJ_TILE == 0 and seq % ATTN_TILE == 0 and dseq == CHUNK
    assert ATTN_TILE % WIN_BLOCK == 0 and WIN_BLOCK == WIN and PROJ_TILE >= BAND
    assert cache_k_win.shape[2] == WIN and cache_k_band.shape[2] == BAND

    h_p, h_s = x_prompt, x_sample
    outs = [[] for _ in range(8)]
    ta = _win_bias_table(WIN_HEADS)
    for i in range(depth):
        pair = lambda g: jnp.tile(g[i], 2)[None, :]
        proj_args = (g_in[i][None, :], w_in[i].astype(_BF16),
                     pair(g_q_win), pair(g_k_win), pair(g_q_band), pair(g_k_band))
        weights = (w_o_win[i].astype(_BF16), w_o_band[i].astype(_BF16), w_out[i].astype(_BF16),
                   g_ple[i][None, :], w_ple_gate[i].astype(_BF16), w_ple[i].astype(_BF16))
        tb = _band_bias_table(rel_bias_band[i], ATTN_TILE)

        *acts, kat, vat, kbt, vbt = _project(h_p, *proj_args, tile=PROJ_TILE, tail_a=WIN, tail_b=BAND)
        h_p = _prompt_attend(acts, h_p, p_prompt[i], ta, tb, sink_win[i], weights)
        outs[0].append(kat.reshape(bsz, WIN, WIN_KV_HEADS, HEAD_DIM))
        outs[1].append(vat.reshape(bsz, WIN, WIN_KV_HEADS, HEAD_DIM))
        outs[2].append(kbt.reshape(bsz, BAND, BAND_HEADS, HEAD_DIM))
        outs[3].append(vbt.reshape(bsz, BAND, BAND_HEADS, HEAD_DIM))

        ntok = dbsz * dseq
        *acts, kan, van, kbn, vbn = _project(h_s.reshape(1, ntok, d), *proj_args,
                                             tile=ntok, tail_a=ntok, tail_b=ntok)
        acts = [a.reshape(dbsz, dseq, a.shape[-1]) for a in acts]
        caches = (_dup_cache(cache_k_win[i]), _dup_cache(cache_v_win[i]),
                  cache_k_band[i].reshape(dbsz, BAND, -1).astype(_BF16),
                  cache_v_band[i].reshape(dbsz, BAND, -1).astype(_BF16))
        h_s = _sample_attend(acts, caches, h_s, p_sample[i], ta, tb, sink_win[i], weights)
        new = lambda t, nh: t.reshape(dbsz, dseq, nh, HEAD_DIM)
        outs[4].append(jnp.concatenate([cache_k_win[i][:, dseq:], new(kan, WIN_KV_HEADS)], axis=1))
        outs[5].append(jnp.concatenate([cache_v_win[i][:, dseq:], new(van, WIN_KV_HEADS)], axis=1))
        outs[6].append(jnp.concatenate([cache_k_band[i][:, dseq:], new(kbn, BAND_HEADS)], axis=1))
        outs[7].append(jnp.concatenate([cache_v_band[i][:, dseq:], new(vbn, BAND_HEADS)], axis=1))
    return (h_p, h_s) + tuple(jnp.stack(o) for o in outs)
```

```python
import functools
import math

import jax
import jax.numpy as jnp
from jax import lax
from jax.experimental import pallas as pl
from jax.experimental.pallas import tpu as pltpu

CHUNK = 64
HEAD_DIM = 64
WIN_HEADS = 8
WIN_KV_HEADS = 2
WIN_CHUNKS = 2
BAND_HEADS = 8
BAND_CHUNKS = 8
REL_CLIP = 128
ALIBI_MAX = 8.0
EPS = 1e-6
NEG_INF = -1e30
LOG2E = math.log2(math.e)

WIN = WIN_CHUNKS * CHUNK
BAND = BAND_CHUNKS * CHUNK
LANES = 128
ATTN_TILE = 256
WIN_BLOCK = 128
QK_LOOKAHEAD = 3
FILL_CHUNK = 256
FILL_FRONT = 2
FILL_FINISH = 3
VMEM_LIMIT_BYTES = 56 * 1024 * 1024

_F32 = jnp.float32
_BF16 = jnp.bfloat16
_NT = (((1,), (1,)), ((), ()))


def _rmsnorm(x, g):
    ms = jnp.mean(x * x, axis=-1, keepdims=True)
    return x * lax.rsqrt(ms + EPS) * g


def _lo_lanes(rows=1):
    return lax.broadcasted_iota(jnp.int32, (rows, LANES), 1) < HEAD_DIM


def _head_rmsnorm(u, g_pair):
    lo = _lo_lanes()
    outs = []
    for j in range(u.shape[1] // LANES):
        t = u[:, j * LANES:(j + 1) * LANES]
        sq = t * t
        s_lo = jnp.sum(jnp.where(lo, sq, 0.0), axis=-1, keepdims=True)
        s_hi = jnp.sum(jnp.where(lo, 0.0, sq), axis=-1, keepdims=True)
        ms = jnp.where(lo, s_lo, s_hi) * (1.0 / HEAD_DIM)
        outs.append(t * lax.rsqrt(ms + EPS) * g_pair)
    return outs[0] if len(outs) == 1 else jnp.concatenate(outs, axis=-1)


def _dup_heads(t):
    lo = _lo_lanes()
    r = pltpu.roll(t, HEAD_DIM, axis=1)
    return jnp.concatenate([jnp.where(lo, t, r), jnp.where(lo, r, t)], axis=-1)


_C_QA, _C_KA, _C_VA, _C_ZA, _C_QB, _C_KB, _C_VB, _C_ZB, _C_GA, _C_GB, _C_END = (
    0, 512, 640, 768, 1280, 1792, 2304, 2816, 3328, 4352, 5376)
_Q_SCALE = HEAD_DIM ** -0.5 * LOG2E


def _proj_kernel(x_ref, gin_ref, w_ref, gqa_ref, gka_ref, gqb_ref, gkb_ref,
                 qa_ref, ka_ref, va_ref, za_ref, qb_ref, kb_ref, vb_ref, zb_ref, ga_ref, gb_ref,
                 kat_ref, vat_ref, kbt_ref, vbt_ref):
    last = pl.program_id(1) == pl.num_programs(1) - 1
    xn = _rmsnorm(x_ref[0], gin_ref[...]).astype(_BF16)

    def proj(a, b):
        return jnp.dot(xn, w_ref[:, a:b], preferred_element_type=_F32)

    qa_ref[0] = (_head_rmsnorm(proj(_C_QA, _C_KA), gqa_ref[...]) * _Q_SCALE).astype(_BF16)
    qb_ref[0] = (_head_rmsnorm(proj(_C_QB, _C_KB), gqb_ref[...]) * _Q_SCALE).astype(_BF16)

    ka = _head_rmsnorm(proj(_C_KA, _C_VA), gka_ref[...])
    va = proj(_C_VA, _C_ZA)
    ka_ref[0] = _dup_heads(ka).astype(_BF16)
    va_ref[0] = _dup_heads(va).astype(_BF16)
    kb = _head_rmsnorm(proj(_C_KB, _C_VB), gkb_ref[...])
    vb = proj(_C_VB, _C_ZB)
    kb_ref[0] = kb.astype(_BF16)
    vb_ref[0] = vb.astype(_BF16)

    za_ref[0] = proj(_C_ZA, _C_QB).astype(_BF16)
    zb_ref[0] = proj(_C_ZB, _C_GA).astype(_BF16)
    ga_ref[0] = proj(_C_GA, _C_GB).astype(_BF16)
    gb_ref[0] = proj(_C_GB, _C_END).astype(_BF16)

    @pl.when(last)
    def _():
        ta, tb = kat_ref.shape[1], kbt_ref.shape[1]
        t = ka.shape[0]
        kat_ref[0] = ka[t - ta:, :]
        vat_ref[0] = va[t - ta:, :]
        kbt_ref[0] = kb[t - tb:, :]
        vbt_ref[0] = vb[t - tb:, :]


def _const_spec(shape, index=None):
    index = (0,) * len(shape) if index is None else index
    return pl.BlockSpec(shape, lambda *_: index, pipeline_mode=pl.Buffered(1))


def _project(x, gin, w_in, gqa, gka, gqb, gkb, *, tile, tail_a, tail_b):
    bsz, seq, d = x.shape
    nt = seq // tile
    row = lambda w: pl.BlockSpec((1, tile, w), lambda b, t: (b, t, 0))
    tail = lambda r, w: pl.BlockSpec((1, r, w), lambda b, t: (b, 0, 0))
    act = lambda w: jax.ShapeDtypeStruct((bsz, seq, w), _BF16)
    widths = (512, 256, 256, 512, 512, 512, 512, 512, 1024, 1024)
    return pl.pallas_call(
        _proj_kernel,
        grid=(bsz, nt),
        in_specs=[row(d), _const_spec((1, d)), _const_spec(w_in.shape),
                  _const_spec((1, LANES)), _const_spec((1, LANES)),
                  _const_spec((1, LANES)), _const_spec((1, LANES))],
        out_specs=[row(w) for w in widths] + [tail(tail_a, 128), tail(tail_a, 128),
                                              tail(tail_b, 512), tail(tail_b, 512)],
        out_shape=[act(w) for w in widths] + [
            jax.ShapeDtypeStruct((bsz, tail_a, 128), _F32),
            jax.ShapeDtypeStruct((bsz, tail_a, 128), _F32),
            jax.ShapeDtypeStruct((bsz, tail_b, 512), _F32),
            jax.ShapeDtypeStruct((bsz, tail_b, 512), _F32)],
        compiler_params=pltpu.CompilerParams(
            dimension_semantics=("arbitrary", "arbitrary"),
            vmem_limit_bytes=VMEM_LIMIT_BYTES),
        name="proj",
    )(x, gin, w_in, gqa, gka, gqb, gkb)


def _win_bias_kernel(o_ref):
    heads, rows, width = o_ref.shape
    r = lax.broadcasted_iota(jnp.int32, (rows, width), 0)
    c = lax.broadcasted_iota(jnp.int32, (rows, width), 1)
    dist = jnp.abs(r + WIN - c).astype(_F32)
    qc, kc = r // CHUNK, c // CHUNK
    valid = jnp.logical_and(kc >= qc, kc <= qc + WIN_CHUNKS)
    for h in range(heads):
        slope = 2.0 ** (-ALIBI_MAX * (h + 1) / heads)
        o_ref[h] = jnp.where(valid, dist * (-slope * LOG2E), NEG_INF)


def _win_bias_table(heads):
    return pl.pallas_call(
        _win_bias_kernel,
        out_shape=jax.ShapeDtypeStruct((heads, WIN_BLOCK, WIN_BLOCK + WIN), _F32),
        name="win_bias",
    )()


def _band_bias_kernel(w_ref, o_ref):
    rows, width = o_ref.shape[1:]
    t = pltpu.roll(jnp.broadcast_to(w_ref[0], (rows, width)), 0, 1, stride=1, stride_axis=0)
    qc = lax.broadcasted_iota(jnp.int32, (rows, width), 0) // CHUNK
    kc = lax.broadcasted_iota(jnp.int32, (rows, width), 1) // CHUNK
    valid = jnp.logical_and(kc >= qc, kc <= qc + BAND_CHUNKS)
    o_ref[0] = jnp.where(valid, t * LOG2E, NEG_INF)


def _band_bias_table(table, rows):
    heads = table.shape[0]
    width = rows + BAND
    assert rows >= 2 * CHUNK and rows % CHUNK == 0
    far = table[:, 2 * REL_CLIP:]
    w = jnp.concatenate([jnp.broadcast_to(far, (heads, BAND - REL_CLIP)), table[:, ::-1],
                         jnp.broadcast_to(far, (heads, width - BAND - REL_CLIP - 1))], axis=1)
    return pl.pallas_call(
        _band_bias_kernel,
        grid=(heads,),
        in_specs=[pl.BlockSpec((1, 1, width), lambda h: (h, 0, 0))],
        out_specs=pl.BlockSpec((1, rows, width), lambda h: (h, 0, 0)),
        out_shape=jax.ShapeDtypeStruct((heads, rows, width), _F32),
        name="band_bias",
    )(w.astype(_F32)[:, None, :])


def _reduce_rows(pieces, elementwise, lane_reduce):
    by_width = {}
    for x in pieces:
        n = x.shape[1]
        by_width[n] = x if n not in by_width else elementwise(by_width[n], x)
    return functools.reduce(elementwise, [lane_reduce(x, axis=-1, keepdims=True)
                                          for x in by_width.values()])


def _attention(tasks, lookahead, fillers=(), front=0):
    def scores(task):
        q, ks, _, bs, _ = task
        return [lax.dot_general(q, k(), _NT, preferred_element_type=_F32) + b()
                for k, b in zip(ks, bs)]

    pending = [scores(t) for t in tasks[:lookahead]]
    outs = []
    filled = 0
    for idx, (_, _, vs, _, sink) in enumerate(tasks):
        if idx + lookahead < len(tasks):
            pending.append(scores(tasks[idx + lookahead]))
        due = front + -(-(idx + 1) * (len(fillers) - front) // len(tasks))
        while filled < min(due, len(fillers)):
            fillers[filled]()
            filled += 1
        s = pending[idx]
        pending[idx] = None
        m = _reduce_rows(s, jnp.maximum, jnp.max)
        if sink is not None:
            m = jnp.maximum(m, sink)
        e = [jnp.exp2(sj - m) for sj in s]
        den = _reduce_rows(e, jnp.add, jnp.sum)
        if sink is not None:
            den = den + jnp.exp2(sink - m)
        o = functools.reduce(jnp.add, [
            jnp.dot(ej.astype(_BF16), v(), preferred_element_type=_F32) for ej, v in zip(e, vs)])
        outs.append(o / den)
    return outs


def _masked_q(qp, half):
    keep = _lo_lanes() if half == 0 else jnp.logical_not(_lo_lanes())
    return jnp.where(keep, qp, jnp.zeros_like(qp))


def _pair_cols(q, h):
    return q[:, (h // 2) * LANES:(h // 2 + 1) * LANES]


def _pair_tiles(per_head):
    lo = _lo_lanes()
    return jnp.concatenate([jnp.where(lo, per_head[2 * p], per_head[2 * p + 1])
                            for p in range(len(per_head) // 2)], axis=-1)


def _finish(oa, ob, za, zb, ga, gb, x, p, woa_ref, wob_ref, wout_ref, gple_ref, wpg_ref, wple_ref,
            fillers=()):
    fillers = list(fillers)
    za = za.astype(_F32)
    zb = zb.astype(_F32)
    br_a = jnp.dot((oa * jax.nn.silu(za)).astype(_BF16), woa_ref[...], preferred_element_type=_F32)
    br_b = jnp.dot((ob * jax.nn.silu(zb)).astype(_BF16), wob_ref[...], preferred_element_type=_F32)
    mrg = jax.nn.sigmoid(ga.astype(_F32)) * br_a + jax.nn.sigmoid(gb.astype(_F32)) * br_b
    h2 = x + jnp.dot(mrg.astype(_BF16), wout_ref[...], preferred_element_type=_F32)
    ple = jnp.dot(p.astype(_BF16), wple_ref[...], preferred_element_type=_F32)
    for f in fillers[:len(fillers) // 2]:
        f()
    gate = jax.nn.sigmoid(jnp.dot(_rmsnorm(h2, gple_ref[...]).astype(_BF16), wpg_ref[...],
                                  preferred_element_type=_F32))
    for f in fillers[len(fillers) // 2:]:
        f()
    return h2 + gate * ple


def _win_col(h):
    return LANES * (h // (WIN_HEADS // WIN_KV_HEADS))


def _band_col(h):
    return LANES * (h // 2)


def _prompt_kernel(xc_ref, xn_ref, p_ref, gin_ref, w_ref, gqa_ref, gka_ref, gqb_ref, gkb_ref,
                   ta_ref, tb_ref, sink_ref,
                   woa_ref, wob_ref, wout_ref, gple_ref, wpg_ref, wple_ref,
                   o_ref, kat_ref, vat_ref, kbt_ref, vbt_ref,
                   q_ref, zg_ref, kbr_ref, vbr_ref, kbn_ref, vbn_ref, kaw_ref, vaw_ref, kan_ref, van_ref):
    i = pl.program_id(1)
    rows = xc_ref.shape[1]
    n_blk = rows // WIN_BLOCK
    ring = kbr_ref.shape[0]
    n_hist = BAND // rows

    def projection(x_ref, slot, kb_dst, vb_dst, ka_dst, va_dst):
        cache = {}

        def proj(a, b):
            if not cache:
                cache["xn"] = _rmsnorm(x_ref[0], gin_ref[...]).astype(_BF16)
            return jnp.dot(cache["xn"], w_ref[:, a:b], preferred_element_type=_F32)

        def q_cols(a, gain_ref, dst):
            q = _head_rmsnorm(proj(a, a + FILL_CHUNK), gain_ref[...]) * _Q_SCALE
            q_ref[slot, :, dst:dst + FILL_CHUNK] = q.astype(_BF16)

        def kb_cols(a):
            k = _head_rmsnorm(proj(a, a + FILL_CHUNK), gkb_ref[...])
            kb_dst[:, a - _C_KB:a - _C_KB + FILL_CHUNK] = k.astype(_BF16)
            kbt_ref[0, :, a - _C_KB:a - _C_KB + FILL_CHUNK] = k

        def vb_cols(a):
            v = proj(a, a + FILL_CHUNK)
            vb_dst[:, a - _C_VB:a - _C_VB + FILL_CHUNK] = v.astype(_BF16)
            vbt_ref[0, :, a - _C_VB:a - _C_VB + FILL_CHUNK] = v

        def kva_cols():
            kva = proj(_C_KA, _C_ZA)
            ka_new = _head_rmsnorm(kva[:, :LANES], gka_ref[...])
            va_new = kva[:, LANES:]
            ka_dst[...] = _dup_heads(ka_new).astype(_BF16)
            va_dst[...] = _dup_heads(va_new).astype(_BF16)
            kat_ref[0] = ka_new[rows - WIN:]
            vat_ref[0] = va_new[rows - WIN:]

        def gate_cols(a, dst):
            zg_ref[slot, :, dst:dst + FILL_CHUNK] = proj(a, a + FILL_CHUNK)

        part = functools.partial
        return ([part(q_cols, a, gqb_ref, a - _C_QB + 512) for a in range(_C_QB, _C_KB, FILL_CHUNK)]
                + [part(kb_cols, a) for a in range(_C_KB, _C_VB, FILL_CHUNK)]
                + [part(q_cols, a, gqa_ref, a - _C_QA) for a in range(_C_QA, _C_KA, FILL_CHUNK)]
                + [kva_cols]
                + [part(vb_cols, a) for a in range(_C_VB, _C_ZB, FILL_CHUNK)]
                + [part(gate_cols, a, a - _C_ZA) for a in range(_C_ZA, _C_QB, FILL_CHUNK)]
                + [part(gate_cols, a, a - _C_ZB + 512) for a in range(_C_ZB, _C_END, FILL_CHUNK)])

    @pl.when(i == 0)
    def _():
        for j in range(1, n_hist + 1):
            kbr_ref[ring - j] = jnp.zeros(kbr_ref.shape[1:], _BF16)
            vbr_ref[ring - j] = jnp.zeros(vbr_ref.shape[1:], _BF16)
        kaw_ref[0:WIN] = jnp.zeros((WIN, kaw_ref.shape[1]), _BF16)
        vaw_ref[0:WIN] = jnp.zeros((WIN, vaw_ref.shape[1]), _BF16)
        for f in projection(xc_ref, 0, kbr_ref.at[0], vbr_ref.at[0], kaw_ref.at[WIN:], vaw_ref.at[WIN:]):
            f()

    def band_bias(h, j):
        b = tb_ref[h, :, j * rows:(j + 1) * rows]
        return b if j == n_hist else jnp.where(i >= n_hist - j, b, NEG_INF)

    def win_bias(h, j):
        b = ta_ref[h]
        if j > 0:
            return b
        return jnp.concatenate([jnp.where(i >= 1, b[:, :WIN], NEG_INF), b[:, WIN:]], axis=-1)

    cur = i % 2
    fillers = projection(xn_ref, 1 - cur, kbn_ref, vbn_ref, kan_ref, van_ref)
    n_late = FILL_FINISH
    tasks = []
    for h in range(BAND_HEADS):
        c = _band_col(h)
        slots = [(i + ring - n_hist + j) % ring for j in range(n_hist + 1)]
        tasks.append((
            _masked_q(q_ref[cur, :, 512 + (h // 2) * LANES:512 + (h // 2 + 1) * LANES], h % 2),
            [functools.partial(lambda s, c: kbr_ref[s, :, c:c + LANES], s, c) for s in slots],
            [functools.partial(lambda s, c: vbr_ref[s, :, c:c + LANES], s, c) for s in slots],
            [functools.partial(band_bias, h, j) for j in range(n_hist + 1)],
            None))
        c = _win_col(h)
        qm = _masked_q(q_ref[cur, :, (h // 2) * LANES:(h // 2 + 1) * LANES], h % 2)
        for j in range(n_blk):
            r0 = j * WIN_BLOCK
            tasks.append((
                qm[r0:r0 + WIN_BLOCK],
                [functools.partial(lambda r0, c: kaw_ref[r0:r0 + WIN_BLOCK + WIN, c:c + LANES], r0, c)],
                [functools.partial(lambda r0, c: vaw_ref[r0:r0 + WIN_BLOCK + WIN, c:c + LANES], r0, c)],
                [functools.partial(win_bias, h, j)],
                sink_ref[h] * LOG2E))
    outs = _attention(tasks, QK_LOOKAHEAD, fillers[:len(fillers) - n_late], FILL_FRONT)
    per = 1 + n_blk
    ob = _pair_tiles([outs[h * per] for h in range(BAND_HEADS)])
    oa = _pair_tiles([jnp.concatenate(outs[h * per + 1:(h + 1) * per], axis=0) for h in range(WIN_HEADS)])
    o_ref[0] = _finish(oa, ob, zg_ref[cur, :, 0:512], zg_ref[cur, :, 512:1024], zg_ref[cur, :, 1024:2048],
                       zg_ref[cur, :, 2048:3072], xc_ref[0], p_ref[0],
                       woa_ref, wob_ref, wout_ref, gple_ref, wpg_ref, wple_ref,
                       fillers[len(fillers) - n_late:])

    kbr_ref[(i + 1) % ring] = kbn_ref[...]
    vbr_ref[(i + 1) % ring] = vbn_ref[...]
    kaw_ref[0:WIN] = kaw_ref[rows:rows + WIN]
    vaw_ref[0:WIN] = vaw_ref[rows:rows + WIN]
    kaw_ref[WIN:] = kan_ref[...]
    vaw_ref[WIN:] = van_ref[...]


def _sample_attn_kernel(qa_ref, qb_ref, kac_ref, kan_ref, vac_ref, van_ref,
                        kbc_ref, kbn_ref, vbc_ref, vbn_ref,
                        za_ref, zb_ref, ga_ref, gb_ref, x_ref, p_ref, ta_ref, tb_ref, sink_ref,
                        woa_ref, wob_ref, wout_ref, gple_ref, wpg_ref, wple_ref, o_ref):
    rows = qa_ref.shape[1]
    half = kbc_ref.shape[1] // 2
    qa, qb = qa_ref[0], qb_ref[0]
    a_edges = (0, kac_ref.shape[1], kac_ref.shape[1] + rows)
    b_edges = (0, half, 2 * half, 2 * half + rows)

    def pieces(cache_ref, new_ref, c, splits):
        n = cache_ref.shape[1] // splits
        return ([functools.partial(lambda j: cache_ref[0, j * n:(j + 1) * n, c:c + LANES], j)
                 for j in range(splits)] + [lambda: new_ref[0, :, c:c + LANES]])

    tasks = []
    for h in range(BAND_HEADS):
        c = _band_col(h)
        tasks.append((_masked_q(_pair_cols(qb, h), h % 2), pieces(kbc_ref, kbn_ref, c, 2), pieces(vbc_ref, vbn_ref, c, 2),
                      [functools.partial(lambda h, j: tb_ref[h, :, b_edges[j]:b_edges[j + 1]], h, j)
                       for j in range(3)], None))
        c = _win_col(h)
        tasks.append((_masked_q(_pair_cols(qa, h), h % 2), pieces(kac_ref, kan_ref, c, 1), pieces(vac_ref, van_ref, c, 1),
                      [functools.partial(lambda h, j: ta_ref[h, :, a_edges[j]:a_edges[j + 1]], h, j)
                       for j in range(2)], sink_ref[h] * LOG2E))
    outs = _attention(tasks, QK_LOOKAHEAD)
    ob = _pair_tiles(outs[0::2])
    oa = _pair_tiles(outs[1::2])
    o_ref[0] = _finish(oa, ob, za_ref[0], zb_ref[0], ga_ref[0], gb_ref[0], x_ref[0], p_ref[0],
                       woa_ref, wob_ref, wout_ref, gple_ref, wpg_ref, wple_ref)


def _weight_specs(d, ple):
    return [_const_spec((512, d)), _const_spec((512, d)), _const_spec((d, d)),
            _const_spec((1, d)), _const_spec((d, d)), _const_spec((ple, d))]


def _prompt_layer(x, p, proj_args, ta, tb, sink, weights):
    gin, w_in, gqa, gka, gqb, gkb = proj_args
    bsz, seq, d = x.shape
    tq = ATTN_TILE
    n_hist = BAND // tq
    n_blocks = seq // tq
    cur = lambda w: pl.BlockSpec((1, tq, w), lambda b, i: (b, i, 0))
    nxt = lambda w: pl.BlockSpec((1, tq, w), lambda b, i: (b, jnp.minimum(i + 1, n_blocks - 1), 0))
    tail = lambda r, w: pl.BlockSpec((1, r, w), lambda b, i: (b, 0, 0))
    btail = pl.BlockSpec((1, tq, 512), lambda b, i: (
        b, jnp.clip(i + 1 - (n_blocks - n_hist), 0, n_hist - 1), 0))
    return pl.pallas_call(
        _prompt_kernel,
        grid=(bsz, n_blocks),
        in_specs=[cur(d), nxt(d), cur(p.shape[-1]), _const_spec((1, d)), _const_spec(w_in.shape),
                  _const_spec((1, LANES)), _const_spec((1, LANES)),
                  _const_spec((1, LANES)), _const_spec((1, LANES)),
                  _const_spec(ta.shape), _const_spec(tb.shape),
                  pl.BlockSpec(memory_space=pltpu.SMEM)]
                 + _weight_specs(d, p.shape[-1]),
        out_specs=[cur(d), tail(WIN, 128), tail(WIN, 128), btail, btail],
        out_shape=[jax.ShapeDtypeStruct((bsz, seq, d), _F32),
                   jax.ShapeDtypeStruct((bsz, WIN, 128), _F32),
                   jax.ShapeDtypeStruct((bsz, WIN, 128), _F32),
                   jax.ShapeDtypeStruct((bsz, BAND, 512), _F32),
                   jax.ShapeDtypeStruct((bsz, BAND, 512), _F32)],
        scratch_shapes=[pltpu.VMEM((2, tq, 1024), _BF16),
                        pltpu.VMEM((2, tq, _C_END - _C_ZB + _C_QB - _C_ZA), _F32),
                        pltpu.VMEM((n_hist + 2, tq, 512), _BF16), pltpu.VMEM((n_hist + 2, tq, 512), _BF16),
                        pltpu.VMEM((tq, 512), _BF16), pltpu.VMEM((tq, 512), _BF16),
                        pltpu.VMEM((WIN + tq, 256), _BF16), pltpu.VMEM((WIN + tq, 256), _BF16),
                        pltpu.VMEM((tq, 256), _BF16), pltpu.VMEM((tq, 256), _BF16)],
        compiler_params=pltpu.CompilerParams(
            dimension_semantics=("arbitrary", "arbitrary"),
            vmem_limit_bytes=VMEM_LIMIT_BYTES),
        name="prompt_layer",
    )(x, x, p, gin, w_in, gqa, gka, gqb, gkb, ta, tb, sink, *weights)


def _sample_attend(acts, caches, x, p, ta, tb, sink, weights):
    qa, ka, va, za, qb, kb, vb, zb, ga, gb = acts
    cka, cva, ckb, cvb = caches
    bsz, seq, d = x.shape
    blk = lambda a: pl.BlockSpec((1,) + a.shape[1:], lambda b: (b, 0, 0))
    ins = (qa, qb, cka, ka, cva, va, ckb, kb, cvb, vb, za, zb, ga, gb, x, p)
    ta_spec = _const_spec((ta.shape[0], seq, ta.shape[2]))
    tb_spec = _const_spec((tb.shape[0], seq, tb.shape[2]))
    return pl.pallas_call(
        _sample_attn_kernel,
        grid=(bsz,),
        in_specs=[blk(a) for a in ins] + [ta_spec, tb_spec, pl.BlockSpec(memory_space=pltpu.SMEM)]
                 + _weight_specs(d, p.shape[-1]),
        out_specs=blk(x),
        out_shape=jax.ShapeDtypeStruct((bsz, seq, d), _F32),
        compiler_params=pltpu.CompilerParams(
            dimension_semantics=("arbitrary",),
            vmem_limit_bytes=VMEM_LIMIT_BYTES),
        name="sample_attn",
    )(*ins, ta, tb, sink, *weights)


def _dup_cache(c):
    return jnp.concatenate([c[:, :, 0], c[:, :, 0], c[:, :, 1], c[:, :, 1]], axis=-1).astype(_BF16)


def kernel(x_prompt, x_sample, cache_k_win, cache_v_win, cache_k_band, cache_v_band, p_prompt, p_sample, g_in, w_in, g_q_win, g_k_win, sink_win, g_q_band, g_k_band, rel_bias_band, w_o_win, w_o_band, w_out, g_ple, w_ple_gate, w_ple):
    depth = w_in.shape[0]
    bsz, seq, d = x_prompt.shape
    dbsz, dseq, _ = x_sample.shape
    assert seq % ATTN_TILE == 0 and seq >= BAND + ATTN_TILE and dseq == CHUNK
    assert ATTN_TILE % WIN_BLOCK == 0 and WIN_BLOCK == WIN and BAND % ATTN_TILE == 0
    assert cache_k_win.shape[2] == WIN and cache_k_band.shape[2] == BAND

    h_p, h_s = x_prompt, x_sample
    outs = [[] for _ in range(8)]
    ta = _win_bias_table(WIN_HEADS)
    for i in range(depth):
        pair = lambda g: jnp.tile(g[i], 2)[None, :]
        proj_args = (g_in[i][None, :], w_in[i].astype(_BF16),
                     pair(g_q_win), pair(g_k_win), pair(g_q_band), pair(g_k_band))
        weights = (w_o_win[i].astype(_BF16), w_o_band[i].astype(_BF16), w_out[i].astype(_BF16),
                   g_ple[i][None, :], w_ple_gate[i].astype(_BF16), w_ple[i].astype(_BF16))
        tb = _band_bias_table(rel_bias_band[i], ATTN_TILE)

        h_p, kat, vat, kbt, vbt = _prompt_layer(h_p, p_prompt[i], proj_args, ta, tb, sink_win[i], weights)
        outs[0].append(kat.reshape(bsz, WIN, WIN_KV_HEADS, HEAD_DIM))
        outs[1].append(vat.reshape(bsz, WIN, WIN_KV_HEADS, HEAD_DIM))
        outs[2].append(kbt.reshape(bsz, BAND, BAND_HEADS, HEAD_DIM))
        outs[3].append(vbt.reshape(bsz, BAND, BAND_HEADS, HEAD_DIM))

        ntok = dbsz * dseq
        *acts, kan, van, kbn, vbn = _project(h_s.reshape(1, ntok, d), *proj_args,
                                             tile=ntok, tail_a=ntok, tail_b=ntok)
        acts = [a.reshape(dbsz, dseq, a.shape[-1]) for a in acts]
        caches = (_dup_cache(cache_k_win[i]), _dup_cache(cache_v_win[i]),
                  cache_k_band[i].reshape(dbsz, BAND, -1).astype(_BF16),
                  cache_v_band[i].reshape(dbsz, BAND, -1).astype(_BF16))
        h_s = _sample_attend(acts, caches, h_s, p_sample[i], ta, tb, sink_win[i], weights)
        new = lambda t, nh: t.reshape(dbsz, dseq, nh, HEAD_DIM)
        outs[4].append(jnp.concatenate([cache_k_win[i][:, dseq:], new(kan, WIN_KV_HEADS)], axis=1))
        outs[5].append(jnp.concatenate([cache_v_win[i][:, dseq:], new(van, WIN_KV_HEADS)], axis=1))
        outs[6].append(jnp.concatenate([cache_k_band[i][:, dseq:], new(kbn, BAND_HEADS)], axis=1))
        outs[7].append(jnp.concatenate([cache_v_band[i][:, dseq:], new(vbn, BAND_HEADS)], axis=1))
    return (h_p, h_s) + tuple(jnp.stack(o) for o in outs)
```

```python
import functools
import math

import jax
import jax.numpy as jnp
from jax import lax
from jax.experimental import pallas as pl
from jax.experimental.pallas import tpu as pltpu

CHUNK = 64
HEAD_DIM = 64
WIN_HEADS = 8
WIN_KV_HEADS = 2
WIN_CHUNKS = 2
BAND_HEADS = 8
BAND_CHUNKS = 8
REL_CLIP = 128
ALIBI_MAX = 8.0
EPS = 1e-6
NEG_INF = -1e30
LOG2E = math.log2(math.e)

WIN = WIN_CHUNKS * CHUNK
BAND = BAND_CHUNKS * CHUNK
LANES = 128
ATTN_TILE = 256
WIN_BLOCK = 128
QK_LOOKAHEAD = 4
FILL_CHUNK = 256
FILL_FRONT = 2
FILL_FINISH = 3
VMEM_LIMIT_BYTES = 56 * 1024 * 1024

_F32 = jnp.float32
_BF16 = jnp.bfloat16
_NT = (((1,), (1,)), ((), ()))


def _rmsnorm(x, g):
    ms = jnp.mean(x * x, axis=-1, keepdims=True)
    return x * lax.rsqrt(ms + EPS) * g


def _lo_lanes(rows=1):
    return lax.broadcasted_iota(jnp.int32, (rows, LANES), 1) < HEAD_DIM


def _head_rmsnorm(u, g_pair):
    lo = _lo_lanes()
    outs = []
    for j in range(u.shape[1] // LANES):
        t = u[:, j * LANES:(j + 1) * LANES]
        sq = t * t
        s_lo = jnp.sum(jnp.where(lo, sq, 0.0), axis=-1, keepdims=True)
        s_hi = jnp.sum(jnp.where(lo, 0.0, sq), axis=-1, keepdims=True)
        ms = jnp.where(lo, s_lo, s_hi) * (1.0 / HEAD_DIM)
        outs.append(t * lax.rsqrt(ms + EPS) * g_pair)
    return outs[0] if len(outs) == 1 else jnp.concatenate(outs, axis=-1)


def _dup_heads(t):
    lo = _lo_lanes()
    r = pltpu.roll(t, HEAD_DIM, axis=1)
    return jnp.concatenate([jnp.where(lo, t, r), jnp.where(lo, r, t)], axis=-1)


_C_QA, _C_KA, _C_VA, _C_ZA, _C_QB, _C_KB, _C_VB, _C_ZB, _C_GA, _C_GB, _C_END = (
    0, 512, 640, 768, 1280, 1792, 2304, 2816, 3328, 4352, 5376)
_Q_SCALE = HEAD_DIM ** -0.5 * LOG2E


def _proj_kernel(x_ref, gin_ref, w_ref, gqa_ref, gka_ref, gqb_ref, gkb_ref,
                 qa_ref, ka_ref, va_ref, za_ref, qb_ref, kb_ref, vb_ref, zb_ref, ga_ref, gb_ref,
                 kat_ref, vat_ref, kbt_ref, vbt_ref):
    last = pl.program_id(1) == pl.num_programs(1) - 1
    xn = _rmsnorm(x_ref[0], gin_ref[...]).astype(_BF16)

    def proj(a, b):
        return jnp.dot(xn, w_ref[:, a:b], preferred_element_type=_F32)

    qa_ref[0] = (_head_rmsnorm(proj(_C_QA, _C_KA), gqa_ref[...]) * _Q_SCALE).astype(_BF16)
    qb_ref[0] = (_head_rmsnorm(proj(_C_QB, _C_KB), gqb_ref[...]) * _Q_SCALE).astype(_BF16)

    ka = _head_rmsnorm(proj(_C_KA, _C_VA), gka_ref[...])
    va = proj(_C_VA, _C_ZA)
    ka_ref[0] = _dup_heads(ka).astype(_BF16)
    va_ref[0] = _dup_heads(va).astype(_BF16)
    kb = _head_rmsnorm(proj(_C_KB, _C_VB), gkb_ref[...])
    vb = proj(_C_VB, _C_ZB)
    kb_ref[0] = kb.astype(_BF16)
    vb_ref[0] = vb.astype(_BF16)

    za_ref[0] = proj(_C_ZA, _C_QB).astype(_BF16)
    zb_ref[0] = proj(_C_ZB, _C_GA).astype(_BF16)
    ga_ref[0] = proj(_C_GA, _C_GB).astype(_BF16)
    gb_ref[0] = proj(_C_GB, _C_END).astype(_BF16)

    @pl.when(last)
    def _():
        ta, tb = kat_ref.shape[1], kbt_ref.shape[1]
        t = ka.shape[0]
        kat_ref[0] = ka[t - ta:, :]
        vat_ref[0] = va[t - ta:, :]
        kbt_ref[0] = kb[t - tb:, :]
        vbt_ref[0] = vb[t - tb:, :]


def _const_spec(shape, index=None):
    index = (0,) * len(shape) if index is None else index
    return pl.BlockSpec(shape, lambda *_: index, pipeline_mode=pl.Buffered(1))


def _project(x, gin, w_in, gqa, gka, gqb, gkb, *, tile, tail_a, tail_b):
    bsz, seq, d = x.shape
    nt = seq // tile
    row = lambda w: pl.BlockSpec((1, tile, w), lambda b, t: (b, t, 0))
    tail = lambda r, w: pl.BlockSpec((1, r, w), lambda b, t: (b, 0, 0))
    act = lambda w: jax.ShapeDtypeStruct((bsz, seq, w), _BF16)
    widths = (512, 256, 256, 512, 512, 512, 512, 512, 1024, 1024)
    return pl.pallas_call(
        _proj_kernel,
        grid=(bsz, nt),
        in_specs=[row(d), _const_spec((1, d)), _const_spec(w_in.shape),
                  _const_spec((1, LANES)), _const_spec((1, LANES)),
                  _const_spec((1, LANES)), _const_spec((1, LANES))],
        out_specs=[row(w) for w in widths] + [tail(tail_a, 128), tail(tail_a, 128),
                                              tail(tail_b, 512), tail(tail_b, 512)],
        out_shape=[act(w) for w in widths] + [
            jax.ShapeDtypeStruct((bsz, tail_a, 128), _F32),
            jax.ShapeDtypeStruct((bsz, tail_a, 128), _F32),
            jax.ShapeDtypeStruct((bsz, tail_b, 512), _F32),
            jax.ShapeDtypeStruct((bsz, tail_b, 512), _F32)],
        compiler_params=pltpu.CompilerParams(
            dimension_semantics=("arbitrary", "arbitrary"),
            vmem_limit_bytes=VMEM_LIMIT_BYTES),
        name="proj",
    )(x, gin, w_in, gqa, gka, gqb, gkb)


def _win_bias_kernel(o_ref):
    heads, rows, width = o_ref.shape
    r = lax.broadcasted_iota(jnp.int32, (rows, width), 0)
    c = lax.broadcasted_iota(jnp.int32, (rows, width), 1)
    dist = jnp.abs(r + WIN - c).astype(_F32)
    qc, kc = r // CHUNK, c // CHUNK
    valid = jnp.logical_and(kc >= qc, kc <= qc + WIN_CHUNKS)
    for h in range(heads):
        slope = 2.0 ** (-ALIBI_MAX * (h + 1) / heads)
        o_ref[h] = jnp.where(valid, dist * (-slope * LOG2E), NEG_INF)


def _win_bias_table(heads):
    return pl.pallas_call(
        _win_bias_kernel,
        out_shape=jax.ShapeDtypeStruct((heads, WIN_BLOCK, WIN_BLOCK + WIN), _F32),
        name="win_bias",
    )()


def _band_bias_kernel(w_ref, o_ref, ot_ref):
    rows, width = o_ref.shape[1:]
    t = pltpu.roll(jnp.broadcast_to(w_ref[0], (rows, width)), 0, 1, stride=1, stride_axis=0)
    qc = lax.broadcasted_iota(jnp.int32, (rows, width), 0) // CHUNK
    kc = lax.broadcasted_iota(jnp.int32, (rows, width), 1) // CHUNK
    valid = jnp.logical_and(kc >= qc, kc <= qc + BAND_CHUNKS)
    bias = jnp.where(valid, t * LOG2E, NEG_INF)
    o_ref[0] = bias
    ot_ref[0] = bias.T


def _band_bias_table(table, rows):
    heads = table.shape[0]
    width = rows + BAND
    assert rows >= 2 * CHUNK and rows % CHUNK == 0
    far = table[:, 2 * REL_CLIP:]
    w = jnp.concatenate([jnp.broadcast_to(far, (heads, BAND - REL_CLIP)), table[:, ::-1],
                         jnp.broadcast_to(far, (heads, width - BAND - REL_CLIP - 1))], axis=1)
    return pl.pallas_call(
        _band_bias_kernel,
        grid=(heads,),
        in_specs=[pl.BlockSpec((1, 1, width), lambda h: (h, 0, 0))],
        out_specs=[pl.BlockSpec((1, rows, width), lambda h: (h, 0, 0)),
                   pl.BlockSpec((1, width, rows), lambda h: (h, 0, 0))],
        out_shape=[jax.ShapeDtypeStruct((heads, rows, width), _F32),
                   jax.ShapeDtypeStruct((heads, width, rows), _F32)],
        name="band_bias",
    )(w.astype(_F32)[:, None, :])


def _reduce_keys(pieces, elementwise, reduce, axis):
    by_size = {}
    for x in pieces:
        n = x.shape[axis]
        by_size[n] = x if n not in by_size else elementwise(by_size[n], x)
    return functools.reduce(elementwise, [reduce(x, axis=axis, keepdims=True)
                                          for x in by_size.values()])


def _attention(tasks, lookahead, fillers=(), front=0):
    def scores(task):
        q, ks, _, bs, _, keys_on_rows = task
        if keys_on_rows:
            return [jnp.dot(k(), q, preferred_element_type=_F32) + b() for k, b in zip(ks, bs)]
        return [lax.dot_general(q, k(), _NT, preferred_element_type=_F32) + b()
                for k, b in zip(ks, bs)]

    pending = [scores(t) for t in tasks[:lookahead]]
    outs = []
    filled = 0
    for idx, (_, _, vs, _, sink, keys_on_rows) in enumerate(tasks):
        if idx + lookahead < len(tasks):
            pending.append(scores(tasks[idx + lookahead]))
        due = front + -(-(idx + 1) * (len(fillers) - front) // len(tasks))
        while filled < min(due, len(fillers)):
            fillers[filled]()
            filled += 1
        s = pending[idx]
        pending[idx] = None
        axis = 0 if keys_on_rows else 1
        m = _reduce_keys(s, jnp.maximum, jnp.max, axis)
        if sink is not None:
            m = jnp.maximum(m, sink)
        e = [jnp.exp2(sj - m) for sj in s]
        den = _reduce_keys(e, jnp.add, jnp.sum, axis)
        if sink is not None:
            den = den + jnp.exp2(sink - m)
        if keys_on_rows:
            pv = [jnp.dot(v(), ej.astype(_BF16), preferred_element_type=_F32) for ej, v in zip(e, vs)]
        else:
            pv = [jnp.dot(ej.astype(_BF16), v(), preferred_element_type=_F32) for ej, v in zip(e, vs)]
        outs.append(functools.reduce(jnp.add, pv) / den)
    return outs


def _masked_q(qp, half):
    keep = _lo_lanes() if half == 0 else jnp.logical_not(_lo_lanes())
    return jnp.where(keep, qp, jnp.zeros_like(qp))


def _masked_rows(qtp, half):
    lo = lax.broadcasted_iota(jnp.int32, qtp.shape, 0) < HEAD_DIM
    return jnp.where(lo if half == 0 else jnp.logical_not(lo), qtp, jnp.zeros_like(qtp))


def _pair_cols(q, h):
    return q[:, (h // 2) * LANES:(h // 2 + 1) * LANES]


def _pair_tiles(per_head):
    lo = _lo_lanes()
    return jnp.concatenate([jnp.where(lo, per_head[2 * p], per_head[2 * p + 1])
                            for p in range(len(per_head) // 2)], axis=-1)


def _finish(oa, ob, za, zb, ga, gb, x, p, woa_ref, wob_ref, wout_ref, gple_ref, wpg_ref, wple_ref,
            fillers=()):
    fillers = list(fillers)
    za = za.astype(_F32)
    zb = zb.astype(_F32)
    br_a = jnp.dot((oa * jax.nn.silu(za)).astype(_BF16), woa_ref[...], preferred_element_type=_F32)
    br_b = jnp.dot((ob * jax.nn.silu(zb)).astype(_BF16), wob_ref[...], preferred_element_type=_F32)
    mrg = jax.nn.sigmoid(ga.astype(_F32)) * br_a + jax.nn.sigmoid(gb.astype(_F32)) * br_b
    h2 = x + jnp.dot(mrg.astype(_BF16), wout_ref[...], preferred_element_type=_F32)
    ple = jnp.dot(p.astype(_BF16), wple_ref[...], preferred_element_type=_F32)
    for f in fillers[:len(fillers) // 2]:
        f()
    gate = jax.nn.sigmoid(jnp.dot(_rmsnorm(h2, gple_ref[...]).astype(_BF16), wpg_ref[...],
                                  preferred_element_type=_F32))
    for f in fillers[len(fillers) // 2:]:
        f()
    return h2 + gate * ple


def _win_col(h):
    return LANES * (h // (WIN_HEADS // WIN_KV_HEADS))


def _band_col(h):
    return LANES * (h // 2)


def _prompt_kernel(xc_ref, xn_ref, p_ref, gin_ref, w_ref, gqa_ref, gka_ref, gqb_ref, gkb_ref,
                   ta_ref, tbt_ref, sink_ref,
                   woa_ref, wob_ref, wout_ref, gple_ref, wpg_ref, wple_ref,
                   o_ref, kat_ref, vat_ref, kbt_ref, vbt_ref,
                   q_ref, qt_ref, zg_ref, kbr_ref, vbr_ref, kbn_ref, vbn_ref,
                   kaw_ref, vaw_ref, kan_ref, van_ref):
    i = pl.program_id(1)
    rows = xc_ref.shape[1]
    n_blk = rows // WIN_BLOCK
    ring = kbr_ref.shape[0]
    n_hist = BAND // rows

    def projection(x_ref, slot, kb_dst, vb_dst, ka_dst, va_dst):
        cache = {}

        def proj(a, b):
            if not cache:
                cache["xn"] = _rmsnorm(x_ref[0], gin_ref[...]).astype(_BF16)
            return jnp.dot(cache["xn"], w_ref[:, a:b], preferred_element_type=_F32)

        def qa_cols(a):
            q = _head_rmsnorm(proj(a, a + FILL_CHUNK), gqa_ref[...]) * _Q_SCALE
            q_ref[slot, :, a - _C_QA:a - _C_QA + FILL_CHUNK] = q.astype(_BF16)

        def qb_cols(a):
            q = _head_rmsnorm(proj(a, a + FILL_CHUNK), gqb_ref[...]) * _Q_SCALE
            qt_ref[slot, a - _C_QB:a - _C_QB + FILL_CHUNK, :] = q.T.astype(_BF16)

        def kb_cols(a):
            k = _head_rmsnorm(proj(a, a + FILL_CHUNK), gkb_ref[...])
            kb_dst[:, a - _C_KB:a - _C_KB + FILL_CHUNK] = k.astype(_BF16)
            kbt_ref[0, :, a - _C_KB:a - _C_KB + FILL_CHUNK] = k

        def vb_cols(a):
            v = proj(a, a + FILL_CHUNK)
            vb_dst[a - _C_VB:a - _C_VB + FILL_CHUNK, :] = v.T.astype(_BF16)
            vbt_ref[0, :, a - _C_VB:a - _C_VB + FILL_CHUNK] = v

        def kva_cols():
            kva = proj(_C_KA, _C_ZA)
            ka_new = _head_rmsnorm(kva[:, :LANES], gka_ref[...])
            va_new = kva[:, LANES:]
            ka_dst[...] = _dup_heads(ka_new).astype(_BF16)
            va_dst[...] = _dup_heads(va_new).astype(_BF16)
            kat_ref[0] = ka_new[rows - WIN:]
            vat_ref[0] = va_new[rows - WIN:]

        def gate_cols(a, dst):
            zg_ref[slot, :, dst:dst + FILL_CHUNK] = proj(a, a + FILL_CHUNK)

        part = functools.partial
        return ([part(qb_cols, a) for a in range(_C_QB, _C_KB, FILL_CHUNK)]
                + [part(kb_cols, a) for a in range(_C_KB, _C_VB, FILL_CHUNK)]
                + [part(qa_cols, a) for a in range(_C_QA, _C_KA, FILL_CHUNK)]
                + [kva_cols]
                + [part(vb_cols, a) for a in range(_C_VB, _C_ZB, FILL_CHUNK)]
                + [part(gate_cols, a, a - _C_ZA) for a in range(_C_ZA, _C_QB, FILL_CHUNK)]
                + [part(gate_cols, a, a - _C_ZB + 512) for a in range(_C_ZB, _C_END, FILL_CHUNK)])

    @pl.when(i == 0)
    def _():
        for j in range(1, n_hist + 1):
            kbr_ref[ring - j] = jnp.zeros(kbr_ref.shape[1:], _BF16)
            vbr_ref[ring - j] = jnp.zeros(vbr_ref.shape[1:], _BF16)
        kaw_ref[0:WIN] = jnp.zeros((WIN, kaw_ref.shape[1]), _BF16)
        vaw_ref[0:WIN] = jnp.zeros((WIN, vaw_ref.shape[1]), _BF16)
        for f in projection(xc_ref, 0, kbr_ref.at[0], vbr_ref.at[0], kaw_ref.at[WIN:], vaw_ref.at[WIN:]):
            f()

    def band_bias(h, j):
        b = tbt_ref[h, j * rows:(j + 1) * rows, :]
        return b if j == n_hist else jnp.where(i >= n_hist - j, b, NEG_INF)

    def win_bias(h, j):
        b = ta_ref[h]
        if j > 0:
            return b
        return jnp.concatenate([jnp.where(i >= 1, b[:, :WIN], NEG_INF), b[:, WIN:]], axis=-1)

    cur = i % 2
    fillers = projection(xn_ref, 1 - cur, kbn_ref, vbn_ref, kan_ref, van_ref)
    n_late = FILL_FINISH
    tasks = []
    for h in range(BAND_HEADS):
        c = _band_col(h)
        slots = [(i + ring - n_hist + j) % ring for j in range(n_hist + 1)]
        tasks.append((
            _masked_rows(qt_ref[cur, c:c + LANES, :], h % 2),
            [functools.partial(lambda s, c: kbr_ref[s, :, c:c + LANES], s, c) for s in slots],
            [functools.partial(lambda s, h: vbr_ref[s, h * HEAD_DIM:(h + 1) * HEAD_DIM, :], s, h) for s in slots],
            [functools.partial(band_bias, h, j) for j in range(n_hist + 1)],
            None, True))
        c = _win_col(h)
        qm = _masked_q(q_ref[cur, :, (h // 2) * LANES:(h // 2 + 1) * LANES], h % 2)
        for j in range(n_blk):
            r0 = j * WIN_BLOCK
            tasks.append((
                qm[r0:r0 + WIN_BLOCK],
                [functools.partial(lambda r0, c: kaw_ref[r0:r0 + WIN_BLOCK + WIN, c:c + LANES], r0, c)],
                [functools.partial(lambda r0, c: vaw_ref[r0:r0 + WIN_BLOCK + WIN, c:c + LANES], r0, c)],
                [functools.partial(win_bias, h, j)],
                sink_ref[h] * LOG2E, False))
    outs = _attention(tasks, QK_LOOKAHEAD, fillers[:len(fillers) - n_late], FILL_FRONT)
    per = 1 + n_blk
    ob = jnp.concatenate([outs[h * per] for h in range(BAND_HEADS)], axis=0).T
    oa = _pair_tiles([jnp.concatenate(outs[h * per + 1:(h + 1) * per], axis=0) for h in range(WIN_HEADS)])
    o_ref[0] = _finish(oa, ob, zg_ref[cur, :, 0:512], zg_ref[cur, :, 512:1024], zg_ref[cur, :, 1024:2048],
                       zg_ref[cur, :, 2048:3072], xc_ref[0], p_ref[0],
                       woa_ref, wob_ref, wout_ref, gple_ref, wpg_ref, wple_ref,
                       fillers[len(fillers) - n_late:])

    kbr_ref[(i + 1) % ring] = kbn_ref[...]
    vbr_ref[(i + 1) % ring] = vbn_ref[...]
    kaw_ref[0:WIN] = kaw_ref[rows:rows + WIN]
    vaw_ref[0:WIN] = vaw_ref[rows:rows + WIN]
    kaw_ref[WIN:] = kan_ref[...]
    vaw_ref[WIN:] = van_ref[...]


def _sample_attn_kernel(qa_ref, qb_ref, kac_ref, kan_ref, vac_ref, van_ref,
                        kbc_ref, kbn_ref, vbc_ref, vbn_ref,
                        za_ref, zb_ref, ga_ref, gb_ref, x_ref, p_ref, ta_ref, tb_ref, sink_ref,
                        woa_ref, wob_ref, wout_ref, gple_ref, wpg_ref, wple_ref, o_ref):
    rows = qa_ref.shape[1]
    half = kbc_ref.shape[1] // 2
    qa, qb = qa_ref[0], qb_ref[0]
    a_edges = (0, kac_ref.shape[1], kac_ref.shape[1] + rows)
    b_edges = (0, half, 2 * half, 2 * half + rows)

    def pieces(cache_ref, new_ref, c, splits):
        n = cache_ref.shape[1] // splits
        return ([functools.partial(lambda j: cache_ref[0, j * n:(j + 1) * n, c:c + LANES], j)
                 for j in range(splits)] + [lambda: new_ref[0, :, c:c + LANES]])

    tasks = []
    for h in range(BAND_HEADS):
        c = _band_col(h)
        tasks.append((_masked_q(_pair_cols(qb, h), h % 2), pieces(kbc_ref, kbn_ref, c, 2), pieces(vbc_ref, vbn_ref, c, 2),
                      [functools.partial(lambda h, j: tb_ref[h, :, b_edges[j]:b_edges[j + 1]], h, j)
                       for j in range(3)], None, False))
        c = _win_col(h)
        tasks.append((_masked_q(_pair_cols(qa, h), h % 2), pieces(kac_ref, kan_ref, c, 1), pieces(vac_ref, van_ref, c, 1),
                      [functools.partial(lambda h, j: ta_ref[h, :, a_edges[j]:a_edges[j + 1]], h, j)
                       for j in range(2)], sink_ref[h] * LOG2E, False))
    outs = _attention(tasks, QK_LOOKAHEAD)
    ob = _pair_tiles(outs[0::2])
    oa = _pair_tiles(outs[1::2])
    o_ref[0] = _finish(oa, ob, za_ref[0], zb_ref[0], ga_ref[0], gb_ref[0], x_ref[0], p_ref[0],
                       woa_ref, wob_ref, wout_ref, gple_ref, wpg_ref, wple_ref)


def _weight_specs(d, ple):
    return [_const_spec((512, d)), _const_spec((512, d)), _const_spec((d, d)),
            _const_spec((1, d)), _const_spec((d, d)), _const_spec((ple, d))]


def _prompt_layer(x, p, proj_args, ta, tbt, sink, weights):
    gin, w_in, gqa, gka, gqb, gkb = proj_args
    bsz, seq, d = x.shape
    tq = ATTN_TILE
    n_hist = BAND // tq
    n_blocks = seq // tq
    cur = lambda w: pl.BlockSpec((1, tq, w), lambda b, i: (b, i, 0))
    nxt = lambda w: pl.BlockSpec((1, tq, w), lambda b, i: (b, jnp.minimum(i + 1, n_blocks - 1), 0))
    tail = lambda r, w: pl.BlockSpec((1, r, w), lambda b, i: (b, 0, 0))
    btail = pl.BlockSpec((1, tq, 512), lambda b, i: (
        b, jnp.clip(i + 1 - (n_blocks - n_hist), 0, n_hist - 1), 0))
    return pl.pallas_call(
        _prompt_kernel,
        grid=(bsz, n_blocks),
        in_specs=[cur(d), nxt(d), cur(p.shape[-1]), _const_spec((1, d)), _const_spec(w_in.shape),
                  _const_spec((1, LANES)), _const_spec((1, LANES)),
                  _const_spec((1, LANES)), _const_spec((1, LANES)),
                  _const_spec(ta.shape), _const_spec(tbt.shape),
                  pl.BlockSpec(memory_space=pltpu.SMEM)]
                 + _weight_specs(d, p.shape[-1]),
        out_specs=[cur(d), tail(WIN, 128), tail(WIN, 128), btail, btail],
        out_shape=[jax.ShapeDtypeStruct((bsz, seq, d), _F32),
                   jax.ShapeDtypeStruct((bsz, WIN, 128), _F32),
                   jax.ShapeDtypeStruct((bsz, WIN, 128), _F32),
                   jax.ShapeDtypeStruct((bsz, BAND, 512), _F32),
                   jax.ShapeDtypeStruct((bsz, BAND, 512), _F32)],
        scratch_shapes=[pltpu.VMEM((2, tq, 512), _BF16), pltpu.VMEM((2, 512, tq), _BF16),
                        pltpu.VMEM((2, tq, _C_END - _C_ZB + _C_QB - _C_ZA), _F32),
                        pltpu.VMEM((n_hist + 2, tq, 512), _BF16), pltpu.VMEM((n_hist + 2, 512, tq), _BF16),
                        pltpu.VMEM((tq, 512), _BF16), pltpu.VMEM((512, tq), _BF16),
                        pltpu.VMEM((WIN + tq, 256), _BF16), pltpu.VMEM((WIN + tq, 256), _BF16),
                        pltpu.VMEM((tq, 256), _BF16), pltpu.VMEM((tq, 256), _BF16)],
        compiler_params=pltpu.CompilerParams(
            dimension_semantics=("arbitrary", "arbitrary"),
            vmem_limit_bytes=VMEM_LIMIT_BYTES),
        name="prompt_layer",
    )(x, x, p, gin, w_in, gqa, gka, gqb, gkb, ta, tbt, sink, *weights)


def _sample_attend(acts, caches, x, p, ta, tb, sink, weights):
    qa, ka, va, za, qb, kb, vb, zb, ga, gb = acts
    cka, cva, ckb, cvb = caches
    bsz, seq, d = x.shape
    blk = lambda a: pl.BlockSpec((1,) + a.shape[1:], lambda b: (b, 0, 0))
    ins = (qa, qb, cka, ka, cva, va, ckb, kb, cvb, vb, za, zb, ga, gb, x, p)
    ta_spec = _const_spec((ta.shape[0], seq, ta.shape[2]))
    tb_spec = _const_spec((tb.shape[0], seq, tb.shape[2]))
    return pl.pallas_call(
        _sample_attn_kernel,
        grid=(bsz,),
        in_specs=[blk(a) for a in ins] + [ta_spec, tb_spec, pl.BlockSpec(memory_space=pltpu.SMEM)]
                 + _weight_specs(d, p.shape[-1]),
        out_specs=blk(x),
        out_shape=jax.ShapeDtypeStruct((bsz, seq, d), _F32),
        compiler_params=pltpu.CompilerParams(
            dimension_semantics=("arbitrary",),
            vmem_limit_bytes=VMEM_LIMIT_BYTES),
        name="sample_attn",
    )(*ins, ta, tb, sink, *weights)


def _dup_cache(c):
    return jnp.concatenate([c[:, :, 0], c[:, :, 0], c[:, :, 1], c[:, :, 1]], axis=-1).astype(_BF16)


def kernel(x_prompt, x_sample, cache_k_win, cache_v_win, cache_k_band, cache_v_band, p_prompt, p_sample, g_in, w_in, g_q_win, g_k_win, sink_win, g_q_band, g_k_band, rel_bias_band, w_o_win, w_o_band, w_out, g_ple, w_ple_gate, w_ple):
    depth = w_in.shape[0]
    bsz, seq, d = x_prompt.shape
    dbsz, dseq, _ = x_sample.shape
    assert seq % ATTN_TILE == 0 and seq >= BAND + ATTN_TILE and dseq == CHUNK
    assert ATTN_TILE % WIN_BLOCK == 0 and WIN_BLOCK == WIN and BAND % ATTN_TILE == 0
    assert cache_k_win.shape[2] == WIN and cache_k_band.shape[2] == BAND

    h_p, h_s = x_prompt, x_sample
    outs = [[] for _ in range(8)]
    ta = _win_bias_table(WIN_HEADS)
    for i in range(depth):
        pair = lambda g: jnp.tile(g[i], 2)[None, :]
        proj_args = (g_in[i][None, :], w_in[i].astype(_BF16),
                     pair(g_q_win), pair(g_k_win), pair(g_q_band), pair(g_k_band))
        weights = (w_o_win[i].astype(_BF16), w_o_band[i].astype(_BF16), w_out[i].astype(_BF16),
                   g_ple[i][None, :], w_ple_gate[i].astype(_BF16), w_ple[i].astype(_BF16))
        tb, tbt = _band_bias_table(rel_bias_band[i], ATTN_TILE)

        h_p, kat, vat, kbt, vbt = _prompt_layer(h_p, p_prompt[i], proj_args, ta, tbt, sink_win[i], weights)
        outs[0].append(kat.reshape(bsz, WIN, WIN_KV_HEADS, HEAD_DIM))
        outs[1].append(vat.reshape(bsz, WIN, WIN_KV_HEADS, HEAD_DIM))
        outs[2].append(kbt.reshape(bsz, BAND, BAND_HEADS, HEAD_DIM))
        outs[3].append(vbt.reshape(bsz, BAND, BAND_HEADS, HEAD_DIM))

        ntok = dbsz * dseq
        *acts, kan, van, kbn, vbn = _project(h_s.reshape(1, ntok, d), *proj_args,
                                             tile=ntok, tail_a=ntok, tail_b=ntok)
        acts = [a.reshape(dbsz, dseq, a.shape[-1]) for a in acts]
        caches = (_dup_cache(cache_k_win[i]), _dup_cache(cache_v_win[i]),
                  cache_k_band[i].reshape(dbsz, BAND, -1).astype(_BF16),
                  cache_v_band[i].reshape(dbsz, BAND, -1).astype(_BF16))
        h_s = _sample_attend(acts, caches, h_s, p_sample[i], ta, tb, sink_win[i], weights)
        new = lambda t, nh: t.reshape(dbsz, dseq, nh, HEAD_DIM)
        outs[4].append(jnp.concatenate([cache_k_win[i][:, dseq:], new(kan, WIN_KV_HEADS)], axis=1))
        outs[5].append(jnp.concatenate([cache_v_win[i][:, dseq:], new(van, WIN_KV_HEADS)], axis=1))
        outs[6].append(jnp.concatenate([cache_k_band[i][:, dseq:], new(kbn, BAND_HEADS)], axis=1))
        outs[7].append(jnp.concatenate([cache_v_band[i][:, dseq:], new(vbn, BAND_HEADS)], axis=1))
    return (h_p, h_s) + tuple(jnp.stack(o) for o in outs)
```

```python
import functools
import math

import jax
import jax.numpy as jnp
from jax import lax
from jax.experimental import pallas as pl
from jax.experimental.pallas import tpu as pltpu

CHUNK = 64
HEAD_DIM = 64
WIN_HEADS = 8
WIN_KV_HEADS = 2
WIN_CHUNKS = 2
BAND_HEADS = 8
BAND_CHUNKS = 8
REL_CLIP = 128
ALIBI_MAX = 8.0
EPS = 1e-6
NEG_INF = -1e30
LOG2E = math.log2(math.e)

WIN = WIN_CHUNKS * CHUNK
BAND = BAND_CHUNKS * CHUNK
LANES = 128
ATTN_TILE = 256
WIN_BLOCK = 128
QK_LOOKAHEAD = 4
FILL_CHUNK = 256
FILL_FRONT = 2
FILL_JOIN = 3
FILL_FINISH = 3
VMEM_LIMIT_BYTES = 56 * 1024 * 1024

_F32 = jnp.float32
_BF16 = jnp.bfloat16
_NT = (((1,), (1,)), ((), ()))


def _rmsnorm(x, g):
    ms = jnp.mean(x * x, axis=-1, keepdims=True)
    return x * lax.rsqrt(ms + EPS) * g


def _lo_lanes(rows=1):
    return lax.broadcasted_iota(jnp.int32, (rows, LANES), 1) < HEAD_DIM


def _head_rmsnorm(u, g_pair):
    lo = _lo_lanes()
    outs = []
    for j in range(u.shape[1] // LANES):
        t = u[:, j * LANES:(j + 1) * LANES]
        sq = t * t
        s_lo = jnp.sum(jnp.where(lo, sq, 0.0), axis=-1, keepdims=True)
        s_hi = jnp.sum(jnp.where(lo, 0.0, sq), axis=-1, keepdims=True)
        ms = jnp.where(lo, s_lo, s_hi) * (1.0 / HEAD_DIM)
        outs.append(t * lax.rsqrt(ms + EPS) * g_pair)
    return outs[0] if len(outs) == 1 else jnp.concatenate(outs, axis=-1)


def _dup_heads(t):
    lo = _lo_lanes()
    r = pltpu.roll(t, HEAD_DIM, axis=1)
    return jnp.concatenate([jnp.where(lo, t, r), jnp.where(lo, r, t)], axis=-1)


_C_QA, _C_KA, _C_VA, _C_ZA, _C_QB, _C_KB, _C_VB, _C_ZB, _C_GA, _C_GB, _C_END = (
    0, 512, 640, 768, 1280, 1792, 2304, 2816, 3328, 4352, 5376)
_Q_SCALE = HEAD_DIM ** -0.5 * LOG2E


def _proj_kernel(x_ref, gin_ref, w_ref, gqa_ref, gka_ref, gqb_ref, gkb_ref,
                 qa_ref, ka_ref, va_ref, za_ref, qb_ref, kb_ref, vb_ref, zb_ref, ga_ref, gb_ref,
                 kat_ref, vat_ref, kbt_ref, vbt_ref):
    last = pl.program_id(1) == pl.num_programs(1) - 1
    xn = _rmsnorm(x_ref[0], gin_ref[...]).astype(_BF16)

    def proj(a, b):
        return jnp.dot(xn, w_ref[:, a:b], preferred_element_type=_F32)

    qa_ref[0] = (_head_rmsnorm(proj(_C_QA, _C_KA), gqa_ref[...]) * _Q_SCALE).astype(_BF16)
    qb_ref[0] = (_head_rmsnorm(proj(_C_QB, _C_KB), gqb_ref[...]) * _Q_SCALE).astype(_BF16)

    ka = _head_rmsnorm(proj(_C_KA, _C_VA), gka_ref[...])
    va = proj(_C_VA, _C_ZA)
    ka_ref[0] = _dup_heads(ka).astype(_BF16)
    va_ref[0] = _dup_heads(va).astype(_BF16)
    kb = _head_rmsnorm(proj(_C_KB, _C_VB), gkb_ref[...])
    vb = proj(_C_VB, _C_ZB)
    kb_ref[0] = kb.astype(_BF16)
    vb_ref[0] = vb.astype(_BF16)

    za_ref[0] = proj(_C_ZA, _C_QB).astype(_BF16)
    zb_ref[0] = proj(_C_ZB, _C_GA).astype(_BF16)
    ga_ref[0] = proj(_C_GA, _C_GB).astype(_BF16)
    gb_ref[0] = proj(_C_GB, _C_END).astype(_BF16)

    @pl.when(last)
    def _():
        ta, tb = kat_ref.shape[1], kbt_ref.shape[1]
        t = ka.shape[0]
        kat_ref[0] = ka[t - ta:, :]
        vat_ref[0] = va[t - ta:, :]
        kbt_ref[0] = kb[t - tb:, :]
        vbt_ref[0] = vb[t - tb:, :]


def _const_spec(shape, index=None):
    index = (0,) * len(shape) if index is None else index
    return pl.BlockSpec(shape, lambda *_: index, pipeline_mode=pl.Buffered(1))


def _project(x, gin, w_in, gqa, gka, gqb, gkb, *, tile, tail_a, tail_b):
    bsz, seq, d = x.shape
    nt = seq // tile
    row = lambda w: pl.BlockSpec((1, tile, w), lambda b, t: (b, t, 0))
    tail = lambda r, w: pl.BlockSpec((1, r, w), lambda b, t: (b, 0, 0))
    act = lambda w: jax.ShapeDtypeStruct((bsz, seq, w), _BF16)
    widths = (512, 256, 256, 512, 512, 512, 512, 512, 1024, 1024)
    return pl.pallas_call(
        _proj_kernel,
        grid=(bsz, nt),
        in_specs=[row(d), _const_spec((1, d)), _const_spec(w_in.shape),
                  _const_spec((1, LANES)), _const_spec((1, LANES)),
                  _const_spec((1, LANES)), _const_spec((1, LANES))],
        out_specs=[row(w) for w in widths] + [tail(tail_a, 128), tail(tail_a, 128),
                                              tail(tail_b, 512), tail(tail_b, 512)],
        out_shape=[act(w) for w in widths] + [
            jax.ShapeDtypeStruct((bsz, tail_a, 128), _F32),
            jax.ShapeDtypeStruct((bsz, tail_a, 128), _F32),
            jax.ShapeDtypeStruct((bsz, tail_b, 512), _F32),
            jax.ShapeDtypeStruct((bsz, tail_b, 512), _F32)],
        compiler_params=pltpu.CompilerParams(
            dimension_semantics=("arbitrary", "arbitrary"),
            vmem_limit_bytes=VMEM_LIMIT_BYTES),
        name="proj",
    )(x, gin, w_in, gqa, gka, gqb, gkb)


def _win_bias_kernel(o_ref):
    heads, rows, width = o_ref.shape
    r = lax.broadcasted_iota(jnp.int32, (rows, width), 0)
    c = lax.broadcasted_iota(jnp.int32, (rows, width), 1)
    dist = jnp.abs(r + WIN - c).astype(_F32)
    qc, kc = r // CHUNK, c // CHUNK
    valid = jnp.logical_and(kc >= qc, kc <= qc + WIN_CHUNKS)
    for h in range(heads):
        slope = 2.0 ** (-ALIBI_MAX * (h + 1) / heads)
        o_ref[h] = jnp.where(valid, dist * (-slope * LOG2E), NEG_INF)


def _win_bias_table(heads):
    return pl.pallas_call(
        _win_bias_kernel,
        out_shape=jax.ShapeDtypeStruct((heads, WIN_BLOCK, WIN_BLOCK + WIN), _F32),
        name="win_bias",
    )()


def _band_bias_kernel(w_ref, o_ref, ot_ref):
    rows, width = o_ref.shape[1:]
    t = pltpu.roll(jnp.broadcast_to(w_ref[0], (rows, width)), 0, 1, stride=1, stride_axis=0)
    qc = lax.broadcasted_iota(jnp.int32, (rows, width), 0) // CHUNK
    kc = lax.broadcasted_iota(jnp.int32, (rows, width), 1) // CHUNK
    valid = jnp.logical_and(kc >= qc, kc <= qc + BAND_CHUNKS)
    bias = jnp.where(valid, t * LOG2E, NEG_INF)
    o_ref[0] = bias
    ot_ref[0] = bias.T


def _band_bias_table(table, rows):
    heads = table.shape[0]
    width = rows + BAND
    assert rows >= 2 * CHUNK and rows % CHUNK == 0
    far = table[:, 2 * REL_CLIP:]
    w = jnp.concatenate([jnp.broadcast_to(far, (heads, BAND - REL_CLIP)), table[:, ::-1],
                         jnp.broadcast_to(far, (heads, width - BAND - REL_CLIP - 1))], axis=1)
    return pl.pallas_call(
        _band_bias_kernel,
        grid=(heads,),
        in_specs=[pl.BlockSpec((1, 1, width), lambda h: (h, 0, 0))],
        out_specs=[pl.BlockSpec((1, rows, width), lambda h: (h, 0, 0)),
                   pl.BlockSpec((1, width, rows), lambda h: (h, 0, 0))],
        out_shape=[jax.ShapeDtypeStruct((heads, rows, width), _F32),
                   jax.ShapeDtypeStruct((heads, width, rows), _F32)],
        name="band_bias",
    )(w.astype(_F32)[:, None, :])


def _reduce_keys(pieces, elementwise, reduce, axis):
    by_size = {}
    for x in pieces:
        n = x.shape[axis]
        by_size[n] = x if n not in by_size else elementwise(by_size[n], x)
    return functools.reduce(elementwise, [reduce(x, axis=axis, keepdims=True)
                                          for x in by_size.values()])


def _attention(tasks, lookahead, fillers=(), front=0):
    def scores(task):
        qf, ks, _, bs, _, keys_on_rows = task
        q = qf()
        if keys_on_rows:
            return [jnp.dot(k(), q, preferred_element_type=_F32) + b() for k, b in zip(ks, bs)]
        return [lax.dot_general(q, k(), _NT, preferred_element_type=_F32) + b()
                for k, b in zip(ks, bs)]

    pending = [scores(t) for t in tasks[:lookahead]]
    outs = []
    filled = 0
    for idx, (_, _, vs, _, sink, keys_on_rows) in enumerate(tasks):
        if idx + lookahead < len(tasks):
            pending.append(scores(tasks[idx + lookahead]))
        due = front + -(-(idx + 1) * (len(fillers) - front) // len(tasks))
        while filled < min(due, len(fillers)):
            fillers[filled]()
            filled += 1
        s = pending[idx]
        pending[idx] = None
        axis = 0 if keys_on_rows else 1
        m = _reduce_keys(s, jnp.maximum, jnp.max, axis)
        if sink is not None:
            m = jnp.maximum(m, sink)
        e = [jnp.exp2(sj - m) for sj in s]
        den = _reduce_keys(e, jnp.add, jnp.sum, axis)
        if sink is not None:
            den = den + jnp.exp2(sink - m)
        if keys_on_rows:
            pv = [jnp.dot(v(), ej.astype(_BF16), preferred_element_type=_F32) for ej, v in zip(e, vs)]
        else:
            pv = [jnp.dot(ej.astype(_BF16), v(), preferred_element_type=_F32) for ej, v in zip(e, vs)]
        outs.append(functools.reduce(jnp.add, pv) / den)
    return outs


def _masked_q(qp, half):
    keep = _lo_lanes() if half == 0 else jnp.logical_not(_lo_lanes())
    return jnp.where(keep, qp, jnp.zeros_like(qp))


def _masked_rows(qtp, half):
    lo = lax.broadcasted_iota(jnp.int32, qtp.shape, 0) < HEAD_DIM
    return jnp.where(lo if half == 0 else jnp.logical_not(lo), qtp, jnp.zeros_like(qtp))


def _pair_cols(q, h):
    return q[:, (h // 2) * LANES:(h // 2 + 1) * LANES]


def _pair_tiles(per_head):
    lo = _lo_lanes()
    return jnp.concatenate([jnp.where(lo, per_head[2 * p], per_head[2 * p + 1])
                            for p in range(len(per_head) // 2)], axis=-1)


def _branch(o, z, w_ref):
    return jnp.dot((o * jax.nn.silu(z.astype(_F32))).astype(_BF16), w_ref[...], preferred_element_type=_F32)


def _finish(br_a, br_b, ga, gb, x, p, wout_ref, gple_ref, wpg_ref, wple_ref, fillers=()):
    fillers = list(fillers)
    mrg = jax.nn.sigmoid(ga.astype(_F32)) * br_a + jax.nn.sigmoid(gb.astype(_F32)) * br_b
    h2 = x + jnp.dot(mrg.astype(_BF16), wout_ref[...], preferred_element_type=_F32)
    ple = jnp.dot(p.astype(_BF16), wple_ref[...], preferred_element_type=_F32)
    for f in fillers[:len(fillers) // 2]:
        f()
    gate = jax.nn.sigmoid(jnp.dot(_rmsnorm(h2, gple_ref[...]).astype(_BF16), wpg_ref[...],
                                  preferred_element_type=_F32))
    for f in fillers[len(fillers) // 2:]:
        f()
    return h2 + gate * ple


def _win_col(h):
    return LANES * (h // (WIN_HEADS // WIN_KV_HEADS))


def _band_col(h):
    return LANES * (h // 2)


def _prompt_kernel(xc_ref, xn_ref, p_ref, gin_ref, w_ref, gqa_ref, gka_ref, gqb_ref, gkb_ref,
                   ta_ref, tbt_ref, sink_ref,
                   woa_ref, wob_ref, wout_ref, gple_ref, wpg_ref, wple_ref,
                   o_ref, kat_ref, vat_ref, kbt_ref, vbt_ref,
                   q_ref, qt_ref, zg_ref, kbr_ref, vbr_ref, kbn_ref, vbn_ref,
                   kaw_ref, vaw_ref, kan_ref, van_ref):
    i = pl.program_id(1)
    rows = xc_ref.shape[1]
    n_blk = rows // WIN_BLOCK
    ring = kbr_ref.shape[0]
    n_hist = BAND // rows

    def projection(x_ref, slot, kb_dst, vb_dst, ka_dst, va_dst):
        cache = {}

        def proj(a, b):
            if not cache:
                cache["xn"] = _rmsnorm(x_ref[0], gin_ref[...]).astype(_BF16)
            return jnp.dot(cache["xn"], w_ref[:, a:b], preferred_element_type=_F32)

        def qa_cols(a):
            q = _head_rmsnorm(proj(a, a + FILL_CHUNK), gqa_ref[...]) * _Q_SCALE
            q_ref[slot, :, a - _C_QA:a - _C_QA + FILL_CHUNK] = q.astype(_BF16)

        def qb_cols(a):
            q = _head_rmsnorm(proj(a, a + FILL_CHUNK), gqb_ref[...]) * _Q_SCALE
            qt_ref[slot, a - _C_QB:a - _C_QB + FILL_CHUNK, :] = q.T.astype(_BF16)

        def kb_cols(a):
            k = _head_rmsnorm(proj(a, a + FILL_CHUNK), gkb_ref[...])
            kb_dst[:, a - _C_KB:a - _C_KB + FILL_CHUNK] = k.astype(_BF16)
            kbt_ref[0, :, a - _C_KB:a - _C_KB + FILL_CHUNK] = k

        def vb_cols(a):
            v = proj(a, a + FILL_CHUNK)
            vb_dst[a - _C_VB:a - _C_VB + FILL_CHUNK, :] = v.T.astype(_BF16)
            vbt_ref[0, :, a - _C_VB:a - _C_VB + FILL_CHUNK] = v

        def kva_cols():
            kva = proj(_C_KA, _C_ZA)
            ka_new = _head_rmsnorm(kva[:, :LANES], gka_ref[...])
            va_new = kva[:, LANES:]
            ka_dst[...] = _dup_heads(ka_new).astype(_BF16)
            va_dst[...] = _dup_heads(va_new).astype(_BF16)
            kat_ref[0] = ka_new[rows - WIN:]
            vat_ref[0] = va_new[rows - WIN:]

        def gate_cols(a, dst):
            zg_ref[slot, :, dst:dst + FILL_CHUNK] = proj(a, a + FILL_CHUNK)

        part = functools.partial
        return ([part(qb_cols, a) for a in range(_C_QB, _C_KB, FILL_CHUNK)]
                + [part(kb_cols, a) for a in range(_C_KB, _C_VB, FILL_CHUNK)]
                + [part(qa_cols, a) for a in range(_C_QA, _C_KA, FILL_CHUNK)]
                + [kva_cols]
                + [part(vb_cols, a) for a in range(_C_VB, _C_ZB, FILL_CHUNK)]
                + [part(gate_cols, a, a - _C_ZA) for a in range(_C_ZA, _C_QB, FILL_CHUNK)]
                + [part(gate_cols, a, a - _C_ZB + 512) for a in range(_C_ZB, _C_END, FILL_CHUNK)])

    @pl.when(i == 0)
    def _():
        for j in range(1, n_hist + 1):
            kbr_ref[ring - j] = jnp.zeros(kbr_ref.shape[1:], _BF16)
            vbr_ref[ring - j] = jnp.zeros(vbr_ref.shape[1:], _BF16)
        kaw_ref[0:WIN] = jnp.zeros((WIN, kaw_ref.shape[1]), _BF16)
        vaw_ref[0:WIN] = jnp.zeros((WIN, vaw_ref.shape[1]), _BF16)
        for f in projection(xc_ref, 0, kbr_ref.at[0], vbr_ref.at[0], kaw_ref.at[WIN:], vaw_ref.at[WIN:]):
            f()

    def band_bias(h, j):
        b = tbt_ref[h, j * rows:(j + 1) * rows, :]
        return b if j == n_hist else jnp.where(i >= n_hist - j, b, NEG_INF)

    def win_bias(h, j):
        b = ta_ref[h]
        if j > 0:
            return b
        return jnp.concatenate([jnp.where(i >= 1, b[:, :WIN], NEG_INF), b[:, WIN:]], axis=-1)

    cur = i % 2
    fillers = projection(xn_ref, 1 - cur, kbn_ref, vbn_ref, kan_ref, van_ref)
    slots = [(i + ring - n_hist + j) % ring for j in range(n_hist + 1)]
    tasks = []
    for h in range(BAND_HEADS):
        c = _band_col(h)
        qtm = _masked_rows(qt_ref[cur, c:c + LANES, :], h % 2)
        tasks.append((
            functools.partial(lambda q: q, qtm),
            [functools.partial(lambda s, c: kbr_ref[s, :, c:c + LANES], s, c) for s in slots],
            [functools.partial(lambda s, h: vbr_ref[s, h * HEAD_DIM:(h + 1) * HEAD_DIM, :], s, h) for s in slots],
            [functools.partial(band_bias, h, j) for j in range(n_hist + 1)],
            None, True))
        c = _win_col(h)
        qm = _masked_q(q_ref[cur, :, (h // 2) * LANES:(h // 2 + 1) * LANES], h % 2)
        for j in range(n_blk):
            r0 = j * WIN_BLOCK
            tasks.append((
                functools.partial(lambda q: q, qm[r0:r0 + WIN_BLOCK]),
                [functools.partial(lambda r0, c: kaw_ref[r0:r0 + WIN_BLOCK + WIN, c:c + LANES], r0, c)],
                [functools.partial(lambda r0, c: vaw_ref[r0:r0 + WIN_BLOCK + WIN, c:c + LANES], r0, c)],
                [functools.partial(win_bias, h, j)],
                sink_ref[h] * LOG2E, False))
    n_attn = len(fillers) - FILL_JOIN - FILL_FINISH
    outs = _attention(tasks, QK_LOOKAHEAD, fillers[:n_attn], FILL_FRONT)
    for f in fillers[n_attn:n_attn + FILL_JOIN]:
        f()
    per = 1 + n_blk
    ob = jnp.concatenate([outs[h * per] for h in range(BAND_HEADS)], axis=0).T
    oa = _pair_tiles([jnp.concatenate(outs[h * per + 1:(h + 1) * per], axis=0) for h in range(WIN_HEADS)])
    br_a = _branch(oa, zg_ref[cur, :, 0:512], woa_ref)
    br_b = _branch(ob, zg_ref[cur, :, 512:1024], wob_ref)
    o_ref[0] = _finish(br_a, br_b, zg_ref[cur, :, 1024:2048], zg_ref[cur, :, 2048:3072],
                       xc_ref[0], p_ref[0], wout_ref, gple_ref, wpg_ref, wple_ref,
                       fillers[n_attn + FILL_JOIN:])

    kbr_ref[(i + 1) % ring] = kbn_ref[...]
    vbr_ref[(i + 1) % ring] = vbn_ref[...]
    kaw_ref[0:WIN] = kaw_ref[rows:rows + WIN]
    vaw_ref[0:WIN] = vaw_ref[rows:rows + WIN]
    kaw_ref[WIN:] = kan_ref[...]
    vaw_ref[WIN:] = van_ref[...]


def _sample_attn_kernel(qa_ref, qb_ref, kan_ref, van_ref, kbn_ref, vbn_ref,
                        ckw_ref, cvw_ref, ckb_ref, cvb_ref, kwf_ref, vwf_ref, kbf_ref, vbf_ref,
                        za_ref, zb_ref, ga_ref, gb_ref, x_ref, p_ref, ta_ref, tb_ref, sink_ref,
                        woa_ref, wob_ref, wout_ref, gple_ref, wpg_ref, wple_ref,
                        o_ref, kws_ref, vws_ref, kbs_ref, vbs_ref):
    rows = qa_ref.shape[1]
    qa, qb = qa_ref[0], qb_ref[0]
    kac, vac = _dup_heads(ckw_ref[0]).astype(_BF16), _dup_heads(cvw_ref[0]).astype(_BF16)
    kbc, vbc = ckb_ref[0].astype(_BF16), cvb_ref[0].astype(_BF16)
    half = kbc.shape[0] // 2
    a_edges = (0, kac.shape[0], kac.shape[0] + rows)
    b_edges = (0, half, 2 * half, 2 * half + rows)

    for out_ref, cache_ref, new_ref in ((kws_ref, ckw_ref, kwf_ref), (vws_ref, cvw_ref, vwf_ref),
                                        (kbs_ref, ckb_ref, kbf_ref), (vbs_ref, cvb_ref, vbf_ref)):
        keep = cache_ref.shape[1] - rows
        out_ref[0, 0:keep] = cache_ref[0, rows:]
        out_ref[0, keep:] = new_ref[0]

    def pieces(cache, new_ref, c, splits):
        n = cache.shape[0] // splits
        return ([functools.partial(lambda j: cache[j * n:(j + 1) * n, c:c + LANES], j)
                 for j in range(splits)] + [lambda: new_ref[0, :, c:c + LANES]])

    tasks = []
    for h in range(BAND_HEADS):
        c = _band_col(h)
        tasks.append((functools.partial(lambda h: _masked_q(_pair_cols(qb, h), h % 2), h),
                      pieces(kbc, kbn_ref, c, 2), pieces(vbc, vbn_ref, c, 2),
                      [functools.partial(lambda h, j: tb_ref[h, :, b_edges[j]:b_edges[j + 1]], h, j)
                       for j in range(3)], None, False))
        c = _win_col(h)
        tasks.append((functools.partial(lambda h: _masked_q(_pair_cols(qa, h), h % 2), h),
                      pieces(kac, kan_ref, c, 1), pieces(vac, van_ref, c, 1),
                      [functools.partial(lambda h, j: ta_ref[h, :, a_edges[j]:a_edges[j + 1]], h, j)
                       for j in range(2)], sink_ref[h] * LOG2E, False))
    outs = _attention(tasks, QK_LOOKAHEAD)
    ob = _pair_tiles(outs[0::2])
    oa = _pair_tiles(outs[1::2])
    o_ref[0] = _finish(_branch(oa, za_ref[0], woa_ref), _branch(ob, zb_ref[0], wob_ref),
                       ga_ref[0], gb_ref[0], x_ref[0], p_ref[0], wout_ref, gple_ref, wpg_ref, wple_ref)


def _weight_specs(d, ple):
    return [_const_spec((512, d)), _const_spec((512, d)), _const_spec((d, d)),
            _const_spec((1, d)), _const_spec((d, d)), _const_spec((ple, d))]


def _prompt_layer(x, p, proj_args, ta, tbt, sink, weights):
    gin, w_in, gqa, gka, gqb, gkb = proj_args
    bsz, seq, d = x.shape
    tq = ATTN_TILE
    n_hist = BAND // tq
    n_blocks = seq // tq
    cur = lambda w: pl.BlockSpec((1, tq, w), lambda b, i: (b, i, 0))
    nxt = lambda w: pl.BlockSpec((1, tq, w), lambda b, i: (b, jnp.minimum(i + 1, n_blocks - 1), 0))
    tail = lambda r, w: pl.BlockSpec((1, r, w), lambda b, i: (b, 0, 0))
    btail = pl.BlockSpec((1, tq, 512), lambda b, i: (
        b, jnp.clip(i + 1 - (n_blocks - n_hist), 0, n_hist - 1), 0))
    return pl.pallas_call(
        _prompt_kernel,
        grid=(bsz, n_blocks),
        in_specs=[cur(d), nxt(d), cur(p.shape[-1]), _const_spec((1, d)), _const_spec(w_in.shape),
                  _const_spec((1, LANES)), _const_spec((1, LANES)),
                  _const_spec((1, LANES)), _const_spec((1, LANES)),
                  _const_spec(ta.shape), _const_spec(tbt.shape),
                  pl.BlockSpec(memory_space=pltpu.SMEM)]
                 + _weight_specs(d, p.shape[-1]),
        out_specs=[cur(d), tail(WIN, 128), tail(WIN, 128), btail, btail],
        out_shape=[jax.ShapeDtypeStruct((bsz, seq, d), _F32),
                   jax.ShapeDtypeStruct((bsz, WIN, 128), _F32),
                   jax.ShapeDtypeStruct((bsz, WIN, 128), _F32),
                   jax.ShapeDtypeStruct((bsz, BAND, 512), _F32),
                   jax.ShapeDtypeStruct((bsz, BAND, 512), _F32)],
        scratch_shapes=[pltpu.VMEM((2, tq, 512), _BF16), pltpu.VMEM((2, 512, tq), _BF16),
                        pltpu.VMEM((2, tq, _C_END - _C_ZB + _C_QB - _C_ZA), _F32),
                        pltpu.VMEM((n_hist + 2, tq, 512), _BF16), pltpu.VMEM((n_hist + 2, 512, tq), _BF16),
                        pltpu.VMEM((tq, 512), _BF16), pltpu.VMEM((512, tq), _BF16),
                        pltpu.VMEM((WIN + tq, 256), _BF16), pltpu.VMEM((WIN + tq, 256), _BF16),
                        pltpu.VMEM((tq, 256), _BF16), pltpu.VMEM((tq, 256), _BF16)],
        compiler_params=pltpu.CompilerParams(
            dimension_semantics=("arbitrary", "arbitrary"),
            vmem_limit_bytes=VMEM_LIMIT_BYTES),
        name="prompt_layer",
    )(x, x, p, gin, w_in, gqa, gka, gqb, gkb, ta, tbt, sink, *weights)


def _sample_attend(acts, new_f32, caches, x, p, ta, tb, sink, weights):
    qa, ka, va, za, qb, kb, vb, zb, ga, gb = acts
    bsz, seq, d = x.shape
    blk = lambda a: pl.BlockSpec((1,) + a.shape[1:], lambda b: (b, 0, 0))
    ins = (qa, qb, ka, va, kb, vb, *caches, *new_f32, za, zb, ga, gb, x, p)
    ta_spec = _const_spec((ta.shape[0], seq, ta.shape[2]))
    tb_spec = _const_spec((tb.shape[0], seq, tb.shape[2]))
    outs = (x,) + tuple(caches)
    return pl.pallas_call(
        _sample_attn_kernel,
        grid=(bsz,),
        in_specs=[blk(a) for a in ins] + [ta_spec, tb_spec, pl.BlockSpec(memory_space=pltpu.SMEM)]
                 + _weight_specs(d, p.shape[-1]),
        out_specs=[blk(a) for a in outs],
        out_shape=[jax.ShapeDtypeStruct(a.shape, _F32) for a in outs],
        compiler_params=pltpu.CompilerParams(
            dimension_semantics=("arbitrary",),
            vmem_limit_bytes=VMEM_LIMIT_BYTES),
        name="sample_attn",
    )(*ins, ta, tb, sink, *weights)


def kernel(x_prompt, x_sample, cache_k_win, cache_v_win, cache_k_band, cache_v_band, p_prompt, p_sample, g_in, w_in, g_q_win, g_k_win, sink_win, g_q_band, g_k_band, rel_bias_band, w_o_win, w_o_band, w_out, g_ple, w_ple_gate, w_ple):
    depth = w_in.shape[0]
    bsz, seq, d = x_prompt.shape
    dbsz, dseq, _ = x_sample.shape
    assert seq % ATTN_TILE == 0 and seq >= BAND + ATTN_TILE and dseq == CHUNK
    assert ATTN_TILE % WIN_BLOCK == 0 and WIN_BLOCK == WIN and BAND % ATTN_TILE == 0
    assert cache_k_win.shape[2] == WIN and cache_k_band.shape[2] == BAND

    h_p, h_s = x_prompt, x_sample
    outs = [[] for _ in range(8)]
    ta = _win_bias_table(WIN_HEADS)
    for i in range(depth):
        pair = lambda g: jnp.tile(g[i], 2)[None, :]
        proj_args = (g_in[i][None, :], w_in[i].astype(_BF16),
                     pair(g_q_win), pair(g_k_win), pair(g_q_band), pair(g_k_band))
        weights = (w_o_win[i].astype(_BF16), w_o_band[i].astype(_BF16), w_out[i].astype(_BF16),
                   g_ple[i][None, :], w_ple_gate[i].astype(_BF16), w_ple[i].astype(_BF16))
        tb, tbt = _band_bias_table(rel_bias_band[i], ATTN_TILE)

        h_p, kat, vat, kbt, vbt = _prompt_layer(h_p, p_prompt[i], proj_args, ta, tbt, sink_win[i], weights)
        outs[0].append(kat.reshape(bsz, WIN, WIN_KV_HEADS, HEAD_DIM))
        outs[1].append(vat.reshape(bsz, WIN, WIN_KV_HEADS, HEAD_DIM))
        outs[2].append(kbt.reshape(bsz, BAND, BAND_HEADS, HEAD_DIM))
        outs[3].append(vbt.reshape(bsz, BAND, BAND_HEADS, HEAD_DIM))

        ntok = dbsz * dseq
        *acts, kan, van, kbn, vbn = _project(h_s.reshape(1, ntok, d), *proj_args,
                                             tile=ntok, tail_a=ntok, tail_b=ntok)
        per_seq = lambda a: a.reshape(dbsz, dseq, a.shape[-1])
        caches = [c[i].reshape(dbsz, c.shape[2], -1)
                  for c in (cache_k_win, cache_v_win, cache_k_band, cache_v_band)]
        h_s, kws, vws, kbs, vbs = _sample_attend(
            [per_seq(a) for a in acts], [per_seq(a) for a in (kan, van, kbn, vbn)], caches,
            h_s, p_sample[i], ta, tb, sink_win[i], weights)
        outs[4].append(kws.reshape(dbsz, WIN, WIN_KV_HEADS, HEAD_DIM))
        outs[5].append(vws.reshape(dbsz, WIN, WIN_KV_HEADS, HEAD_DIM))
        outs[6].append(kbs.reshape(dbsz, BAND, BAND_HEADS, HEAD_DIM))
        outs[7].append(vbs.reshape(dbsz, BAND, BAND_HEADS, HEAD_DIM))
    return (h_p, h_s) + tuple(jnp.stack(o) for o in outs)
```

```python
import functools
import math

import jax
import jax.numpy as jnp
from jax import lax
from jax.experimental import pallas as pl
from jax.experimental.pallas import tpu as pltpu

CHUNK = 64
HEAD_DIM = 64
WIN_HEADS = 8
WIN_KV_HEADS = 2
WIN_CHUNKS = 2
BAND_HEADS = 8
BAND_CHUNKS = 8
REL_CLIP = 128
ALIBI_MAX = 8.0
EPS = 1e-6
NEG_INF = -1e30
LOG2E = math.log2(math.e)

WIN = WIN_CHUNKS * CHUNK
BAND = BAND_CHUNKS * CHUNK
LANES = 128
ATTN_TILE = 256
WIN_BLOCK = 128
QK_LOOKAHEAD = 3
FILL_CHUNK = 256
FILL_FRONT = 2
FILL_FINISH = 3
VMEM_LIMIT_BYTES = 56 * 1024 * 1024

_F32 = jnp.float32
_BF16 = jnp.bfloat16
_NT = (((1,), (1,)), ((), ()))


def _rmsnorm(x, g):
    ms = jnp.mean(x * x, axis=-1, keepdims=True)
    return x * lax.rsqrt(ms + EPS) * g


def _lo_lanes(rows=1):
    return lax.broadcasted_iota(jnp.int32, (rows, LANES), 1) < HEAD_DIM


def _head_rmsnorm(u, g_pair):
    lo = _lo_lanes()
    outs = []
    for j in range(u.shape[1] // LANES):
        t = u[:, j * LANES:(j + 1) * LANES]
        sq = t * t
        s_lo = jnp.sum(jnp.where(lo, sq, 0.0), axis=-1, keepdims=True)
        s_hi = jnp.sum(jnp.where(lo, 0.0, sq), axis=-1, keepdims=True)
        ms = jnp.where(lo, s_lo, s_hi) * (1.0 / HEAD_DIM)
        outs.append(t * lax.rsqrt(ms + EPS) * g_pair)
    return outs[0] if len(outs) == 1 else jnp.concatenate(outs, axis=-1)


def _dup_heads(t):
    lo = _lo_lanes()
    r = pltpu.roll(t, HEAD_DIM, axis=1)
    return jnp.concatenate([jnp.where(lo, t, r), jnp.where(lo, r, t)], axis=-1)


_C_QA, _C_KA, _C_VA, _C_ZA, _C_QB, _C_KB, _C_VB, _C_ZB, _C_GA, _C_GB, _C_END = (
    0, 512, 640, 768, 1280, 1792, 2304, 2816, 3328, 4352, 5376)
_Q_SCALE = HEAD_DIM ** -0.5 * LOG2E


def _proj_kernel(x_ref, gin_ref, w_ref, gqa_ref, gka_ref, gqb_ref, gkb_ref,
                 qa_ref, ka_ref, va_ref, za_ref, qb_ref, kb_ref, vb_ref, zb_ref, ga_ref, gb_ref,
                 kat_ref, vat_ref, kbt_ref, vbt_ref):
    last = pl.program_id(1) == pl.num_programs(1) - 1
    xn = _rmsnorm(x_ref[0], gin_ref[...]).astype(_BF16)

    def proj(a, b):
        return jnp.dot(xn, w_ref[:, a:b], preferred_element_type=_F32)

    qa_ref[0] = (_head_rmsnorm(proj(_C_QA, _C_KA), gqa_ref[...]) * _Q_SCALE).astype(_BF16)
    qb_ref[0] = (_head_rmsnorm(proj(_C_QB, _C_KB), gqb_ref[...]) * _Q_SCALE).astype(_BF16)

    ka = _head_rmsnorm(proj(_C_KA, _C_VA), gka_ref[...])
    va = proj(_C_VA, _C_ZA)
    ka_ref[0] = _dup_heads(ka).astype(_BF16)
    va_ref[0] = _dup_heads(va).astype(_BF16)
    kb = _head_rmsnorm(proj(_C_KB, _C_VB), gkb_ref[...])
    vb = proj(_C_VB, _C_ZB)
    kb_ref[0] = kb.astype(_BF16)
    vb_ref[0] = vb.astype(_BF16)

    za_ref[0] = proj(_C_ZA, _C_QB).astype(_BF16)
    zb_ref[0] = proj(_C_ZB, _C_GA).astype(_BF16)
    ga_ref[0] = proj(_C_GA, _C_GB).astype(_BF16)
    gb_ref[0] = proj(_C_GB, _C_END).astype(_BF16)

    @pl.when(last)
    def _():
        ta, tb = kat_ref.shape[1], kbt_ref.shape[1]
        t = ka.shape[0]
        kat_ref[0] = ka[t - ta:, :]
        vat_ref[0] = va[t - ta:, :]
        kbt_ref[0] = kb[t - tb:, :]
        vbt_ref[0] = vb[t - tb:, :]


def _const_spec(shape, index=None):
    index = (0,) * len(shape) if index is None else index
    return pl.BlockSpec(shape, lambda *_: index, pipeline_mode=pl.Buffered(1))


def _project(x, gin, w_in, gqa, gka, gqb, gkb, *, tile, tail_a, tail_b):
    bsz, seq, d = x.shape
    nt = seq // tile
    row = lambda w: pl.BlockSpec((1, tile, w), lambda b, t: (b, t, 0))
    tail = lambda r, w: pl.BlockSpec((1, r, w), lambda b, t: (b, 0, 0))
    act = lambda w: jax.ShapeDtypeStruct((bsz, seq, w), _BF16)
    widths = (512, 256, 256, 512, 512, 512, 512, 512, 1024, 1024)
    return pl.pallas_call(
        _proj_kernel,
        grid=(bsz, nt),
        in_specs=[row(d), _const_spec((1, d)), _const_spec(w_in.shape),
                  _const_spec((1, LANES)), _const_spec((1, LANES)),
                  _const_spec((1, LANES)), _const_spec((1, LANES))],
        out_specs=[row(w) for w in widths] + [tail(tail_a, 128), tail(tail_a, 128),
                                              tail(tail_b, 512), tail(tail_b, 512)],
        out_shape=[act(w) for w in widths] + [
            jax.ShapeDtypeStruct((bsz, tail_a, 128), _F32),
            jax.ShapeDtypeStruct((bsz, tail_a, 128), _F32),
            jax.ShapeDtypeStruct((bsz, tail_b, 512), _F32),
            jax.ShapeDtypeStruct((bsz, tail_b, 512), _F32)],
        compiler_params=pltpu.CompilerParams(
            dimension_semantics=("arbitrary", "arbitrary"),
            vmem_limit_bytes=VMEM_LIMIT_BYTES),
        name="proj",
    )(x, gin, w_in, gqa, gka, gqb, gkb)


def _win_bias_kernel(o_ref, ot_ref):
    heads, rows, width = o_ref.shape
    r = lax.broadcasted_iota(jnp.int32, (rows, width), 0)
    c = lax.broadcasted_iota(jnp.int32, (rows, width), 1)
    dist = jnp.abs(r + WIN - c).astype(_F32)
    qc, kc = r // CHUNK, c // CHUNK
    valid = jnp.logical_and(kc >= qc, kc <= qc + WIN_CHUNKS)
    for h in range(heads):
        slope = 2.0 ** (-ALIBI_MAX * (h + 1) / heads)
        bias = jnp.where(valid, dist * (-slope * LOG2E), NEG_INF)
        o_ref[h] = bias
        ot_ref[h // 2, :, (h % 2) * rows:(h % 2 + 1) * rows] = bias.T


def _win_bias_table(heads):
    return pl.pallas_call(
        _win_bias_kernel,
        out_shape=[jax.ShapeDtypeStruct((heads, WIN_BLOCK, WIN_BLOCK + WIN), _F32),
                   jax.ShapeDtypeStruct((heads // 2, WIN_BLOCK + WIN, 2 * WIN_BLOCK), _F32)],
        name="win_bias",
    )()


def _band_bias_kernel(w_ref, o_ref, ot_ref):
    width, rows = ot_ref.shape[1:]
    t = pltpu.roll(jnp.broadcast_to(w_ref[0], (rows, width)), 0, 1, stride=1, stride_axis=0)
    qc = lax.broadcasted_iota(jnp.int32, (rows, width), 0) // CHUNK
    kc = lax.broadcasted_iota(jnp.int32, (rows, width), 1) // CHUNK
    valid = jnp.logical_and(kc >= qc, kc <= qc + BAND_CHUNKS)
    bias = jnp.where(valid, t * LOG2E, NEG_INF)
    o_ref[0] = bias[:o_ref.shape[1]]
    ot_ref[0] = bias.T


def _band_bias_table(table, rows):
    heads = table.shape[0]
    width = rows + BAND
    assert rows >= 2 * CHUNK and rows % CHUNK == 0
    far = table[:, 2 * REL_CLIP:]
    w = jnp.concatenate([jnp.broadcast_to(far, (heads, BAND - REL_CLIP)), table[:, ::-1],
                         jnp.broadcast_to(far, (heads, width - BAND - REL_CLIP - 1))], axis=1)
    return pl.pallas_call(
        _band_bias_kernel,
        grid=(heads,),
        in_specs=[pl.BlockSpec((1, 1, width), lambda h: (h, 0, 0))],
        out_specs=[pl.BlockSpec((1, CHUNK, width), lambda h: (h, 0, 0)),
                   pl.BlockSpec((1, width, rows), lambda h: (h, 0, 0))],
        out_shape=[jax.ShapeDtypeStruct((heads, CHUNK, width), _F32),
                   jax.ShapeDtypeStruct((heads, width, rows), _F32)],
        name="band_bias",
    )(w.astype(_F32)[:, None, :])


def _reduce_keys(pieces, elementwise, reduce, axis):
    by_size = {}
    for x in pieces:
        n = x.shape[axis]
        by_size[n] = x if n not in by_size else elementwise(by_size[n], x)
    return functools.reduce(elementwise, [reduce(x, axis=axis, keepdims=True)
                                          for x in by_size.values()])


def _attention(tasks, lookahead, fillers=(), front=0):
    def scores(task):
        q, ks, _, bs, _, keys_on_rows = task
        if keys_on_rows:
            return [jnp.dot(k(), q, preferred_element_type=_F32) + b() for k, b in zip(ks, bs)]
        return [lax.dot_general(q, k(), _NT, preferred_element_type=_F32) + b()
                for k, b in zip(ks, bs)]

    pending = [scores(t) for t in tasks[:lookahead]]
    outs = []
    filled = 0
    for idx, (_, _, vs, _, sink, keys_on_rows) in enumerate(tasks):
        if idx + lookahead < len(tasks):
            pending.append(scores(tasks[idx + lookahead]))
        due = front + -(-(idx + 1) * (len(fillers) - front) // len(tasks))
        while filled < min(due, len(fillers)):
            fillers[filled]()
            filled += 1
        s = pending[idx]
        pending[idx] = None
        axis = 0 if keys_on_rows else 1
        m = _reduce_keys(s, jnp.maximum, jnp.max, axis)
        if sink is not None:
            m = jnp.maximum(m, sink)
        e = [jnp.exp2(sj - m) for sj in s]
        den = _reduce_keys(e, jnp.add, jnp.sum, axis)
        if sink is not None:
            den = den + jnp.exp2(sink - m)
        if keys_on_rows:
            pv = [jnp.dot(v(), ej.astype(_BF16), preferred_element_type=_F32) for ej, v in zip(e, vs)]
        else:
            pv = [jnp.dot(ej.astype(_BF16), v(), preferred_element_type=_F32) for ej, v in zip(e, vs)]
        outs.append(functools.reduce(jnp.add, pv) / den)
    return outs


def _masked_q(qp, half):
    keep = _lo_lanes() if half == 0 else jnp.logical_not(_lo_lanes())
    return jnp.where(keep, qp, jnp.zeros_like(qp))


def _masked_rows(qtp, half):
    lo = lax.broadcasted_iota(jnp.int32, qtp.shape, 0) < HEAD_DIM
    return jnp.where(lo if half == 0 else jnp.logical_not(lo), qtp, jnp.zeros_like(qtp))


def _pair_cols(q, h):
    return q[:, (h // 2) * LANES:(h // 2 + 1) * LANES]


def _pair_tiles(per_head):
    lo = _lo_lanes()
    return jnp.concatenate([jnp.where(lo, per_head[2 * p], per_head[2 * p + 1])
                            for p in range(len(per_head) // 2)], axis=-1)


def _finish(oa, ob, za, zb, ga, gb, x, p, woa_ref, wob_ref, wout_ref, gple_ref, wpg_ref, wple_ref,
            fillers=()):
    fillers = list(fillers)
    za = za.astype(_F32)
    zb = zb.astype(_F32)
    br_a = jnp.dot((oa * jax.nn.silu(za)).astype(_BF16), woa_ref[...], preferred_element_type=_F32)
    br_b = jnp.dot((ob * jax.nn.silu(zb)).astype(_BF16), wob_ref[...], preferred_element_type=_F32)
    mrg = jax.nn.sigmoid(ga.astype(_F32)) * br_a + jax.nn.sigmoid(gb.astype(_F32)) * br_b
    h2 = x + jnp.dot(mrg.astype(_BF16), wout_ref[...], preferred_element_type=_F32)
    ple = jnp.dot(p.astype(_BF16), wple_ref[...], preferred_element_type=_F32)
    for f in fillers[:len(fillers) // 2]:
        f()
    gate = jax.nn.sigmoid(jnp.dot(_rmsnorm(h2, gple_ref[...]).astype(_BF16), wpg_ref[...],
                                  preferred_element_type=_F32))
    for f in fillers[len(fillers) // 2:]:
        f()
    return h2 + gate * ple


def _win_col(h):
    return LANES * (h // (WIN_HEADS // WIN_KV_HEADS))


def _band_col(h):
    return LANES * (h // 2)


def _prompt_kernel(xc_ref, xn_ref, p_ref, gin_ref, w_ref, gqa_ref, gka_ref, gqb_ref, gkb_ref,
                   tat_ref, tbt_ref, sink_ref,
                   woa_ref, wob_ref, wout_ref, gple_ref, wpg_ref, wple_ref,
                   o_ref, kat_ref, vat_ref, kbt_ref, vbt_ref,
                   qat_ref, qbt_ref, zg_ref, kbr_ref, vbr_ref, kbn_ref, vbn_ref,
                   kaw_ref, vaw_ref, kan_ref, van_ref):
    i = pl.program_id(1)
    rows = xc_ref.shape[1]
    n_blk = rows // WIN_BLOCK
    ring = kbr_ref.shape[0]
    n_hist = BAND // rows

    def projection(x_ref, slot, kb_dst, vb_dst, ka_dst, va_dst):
        cache = {}

        def proj(a, b):
            if not cache:
                cache["xn"] = _rmsnorm(x_ref[0], gin_ref[...]).astype(_BF16)
            return jnp.dot(cache["xn"], w_ref[:, a:b], preferred_element_type=_F32)

        def qa_cols(a):
            q = _head_rmsnorm(proj(a, a + FILL_CHUNK), gqa_ref[...]) * _Q_SCALE
            qat_ref[slot, a - _C_QA:a - _C_QA + FILL_CHUNK, :] = q.T.astype(_BF16)

        def qb_cols(a):
            q = _head_rmsnorm(proj(a, a + FILL_CHUNK), gqb_ref[...]) * _Q_SCALE
            qbt_ref[slot, a - _C_QB:a - _C_QB + FILL_CHUNK, :] = q.T.astype(_BF16)

        def kb_cols(a):
            k = _head_rmsnorm(proj(a, a + FILL_CHUNK), gkb_ref[...])
            kb_dst[:, a - _C_KB:a - _C_KB + FILL_CHUNK] = k.astype(_BF16)
            kbt_ref[0, :, a - _C_KB:a - _C_KB + FILL_CHUNK] = k

        def vb_cols(a):
            v = proj(a, a + FILL_CHUNK)
            vb_dst[a - _C_VB:a - _C_VB + FILL_CHUNK, :] = v.T.astype(_BF16)
            vbt_ref[0, :, a - _C_VB:a - _C_VB + FILL_CHUNK] = v

        def kva_cols():
            kva = proj(_C_KA, _C_ZA)
            ka_new = _head_rmsnorm(kva[:, :LANES], gka_ref[...])
            va_new = kva[:, LANES:]
            ka_dst[...] = _dup_heads(ka_new).astype(_BF16)
            va_dst[...] = va_new.T.astype(_BF16)
            kat_ref[0] = ka_new[rows - WIN:]
            vat_ref[0] = va_new[rows - WIN:]

        def gate_cols(a, dst):
            zg_ref[slot, :, dst:dst + FILL_CHUNK] = proj(a, a + FILL_CHUNK)

        part = functools.partial
        return ([part(qb_cols, a) for a in range(_C_QB, _C_KB, FILL_CHUNK)]
                + [part(kb_cols, a) for a in range(_C_KB, _C_VB, FILL_CHUNK)]
                + [part(qa_cols, a) for a in range(_C_QA, _C_KA, FILL_CHUNK)]
                + [kva_cols]
                + [part(vb_cols, a) for a in range(_C_VB, _C_ZB, FILL_CHUNK)]
                + [part(gate_cols, a, a - _C_ZA) for a in range(_C_ZA, _C_QB, FILL_CHUNK)]
                + [part(gate_cols, a, a - _C_ZB + 512) for a in range(_C_ZB, _C_END, FILL_CHUNK)])

    @pl.when(i == 0)
    def _():
        for j in range(1, n_hist + 1):
            kbr_ref[ring - j] = jnp.zeros(kbr_ref.shape[1:], _BF16)
            vbr_ref[ring - j] = jnp.zeros(vbr_ref.shape[1:], _BF16)
        kaw_ref[0:WIN] = jnp.zeros((WIN, kaw_ref.shape[1]), _BF16)
        vaw_ref[:, 0:WIN] = jnp.zeros((vaw_ref.shape[0], WIN), _BF16)
        for f in projection(xc_ref, 0, kbr_ref.at[0], vbr_ref.at[0], kaw_ref.at[WIN:], vaw_ref.at[:, WIN:]):
            f()

    def band_bias(h, j):
        b = tbt_ref[h, j * rows:(j + 1) * rows, :]
        return b if j == n_hist else jnp.where(i >= n_hist - j, b, NEG_INF)

    def win_bias(pair, j):
        b = tat_ref[pair]
        if j > 0:
            return b
        return jnp.concatenate([jnp.where(i >= 1, b[:WIN], NEG_INF), b[WIN:]], axis=0)

    cur = i % 2
    fillers = projection(xn_ref, 1 - cur, kbn_ref, vbn_ref, kan_ref, van_ref)
    n_late = FILL_FINISH
    slots = [(i + ring - n_hist + j) % ring for j in range(n_hist + 1)]
    heads_per_kv = WIN_HEADS // WIN_KV_HEADS
    lane = lax.broadcasted_iota(jnp.int32, (1, 2 * WIN_BLOCK), 1)
    tasks = []
    for h in range(BAND_HEADS):
        c = _band_col(h)
        tasks.append((
            _masked_rows(qbt_ref[cur, c:c + LANES, :], h % 2),
            [functools.partial(lambda s, c: kbr_ref[s, :, c:c + LANES], s, c) for s in slots],
            [functools.partial(lambda s, h: vbr_ref[s, h * HEAD_DIM:(h + 1) * HEAD_DIM, :], s, h) for s in slots],
            [functools.partial(band_bias, h, j) for j in range(n_hist + 1)],
            None, True))
        pair, j = h // n_blk, h % n_blk
        kv = 2 * pair // heads_per_kv
        r0 = j * WIN_BLOCK
        qtp = qat_ref[cur, pair * LANES:(pair + 1) * LANES, r0:r0 + WIN_BLOCK]
        tasks.append((
            jnp.concatenate([_masked_rows(qtp, 0), _masked_rows(qtp, 1)], axis=1),
            [functools.partial(lambda r0, kv: kaw_ref[r0:r0 + WIN_BLOCK + WIN, kv * LANES:(kv + 1) * LANES], r0, kv)],
            [functools.partial(lambda r0, kv: vaw_ref[kv * HEAD_DIM:(kv + 1) * HEAD_DIM, r0:r0 + WIN_BLOCK + WIN], r0, kv)],
            [functools.partial(win_bias, pair, j)],
            jnp.where(lane < WIN_BLOCK, sink_ref[2 * pair], sink_ref[2 * pair + 1]) * LOG2E, True))
    outs = _attention(tasks, QK_LOOKAHEAD, fillers[:len(fillers) - n_late], FILL_FRONT)
    ob = jnp.concatenate(outs[0::2], axis=0).T
    wouts = outs[1::2]
    oa = jnp.concatenate(
        [jnp.concatenate([wouts[(h // 2) * n_blk + j][:, (h % 2) * WIN_BLOCK:(h % 2 + 1) * WIN_BLOCK]
                          for j in range(n_blk)], axis=1) for h in range(WIN_HEADS)], axis=0).T
    o_ref[0] = _finish(oa, ob, zg_ref[cur, :, 0:512], zg_ref[cur, :, 512:1024], zg_ref[cur, :, 1024:2048],
                       zg_ref[cur, :, 2048:3072], xc_ref[0], p_ref[0],
                       woa_ref, wob_ref, wout_ref, gple_ref, wpg_ref, wple_ref,
                       fillers[len(fillers) - n_late:])

    kbr_ref[(i + 1) % ring] = kbn_ref[...]
    vbr_ref[(i + 1) % ring] = vbn_ref[...]
    kaw_ref[0:WIN] = kaw_ref[rows:rows + WIN]
    vaw_ref[:, 0:WIN] = vaw_ref[:, rows:rows + WIN]
    kaw_ref[WIN:] = kan_ref[...]
    vaw_ref[:, WIN:] = van_ref[...]


def _sample_attn_kernel(qa_ref, qb_ref, kac_ref, kan_ref, vac_ref, van_ref,
                        kbc_ref, kbn_ref, vbc_ref, vbn_ref,
                        za_ref, zb_ref, ga_ref, gb_ref, x_ref, p_ref, ta_ref, tb_ref, sink_ref,
                        woa_ref, wob_ref, wout_ref, gple_ref, wpg_ref, wple_ref, o_ref):
    rows = qa_ref.shape[1]
    half = kbc_ref.shape[1] // 2
    qa, qb = qa_ref[0], qb_ref[0]
    a_edges = (0, kac_ref.shape[1], kac_ref.shape[1] + rows)
    b_edges = (0, half, 2 * half, 2 * half + rows)

    def pieces(cache_ref, new_ref, c, splits):
        n = cache_ref.shape[1] // splits
        return ([functools.partial(lambda j: cache_ref[0, j * n:(j + 1) * n, c:c + LANES], j)
                 for j in range(splits)] + [lambda: new_ref[0, :, c:c + LANES]])

    tasks = []
    for h in range(BAND_HEADS):
        c = _band_col(h)
        tasks.append((_masked_q(_pair_cols(qb, h), h % 2), pieces(kbc_ref, kbn_ref, c, 2), pieces(vbc_ref, vbn_ref, c, 2),
                      [functools.partial(lambda h, j: tb_ref[h, :, b_edges[j]:b_edges[j + 1]], h, j)
                       for j in range(3)], None, False))
        c = _win_col(h)
        tasks.append((_masked_q(_pair_cols(qa, h), h % 2), pieces(kac_ref, kan_ref, c, 1), pieces(vac_ref, van_ref, c, 1),
                      [functools.partial(lambda h, j: ta_ref[h, :, a_edges[j]:a_edges[j + 1]], h, j)
                       for j in range(2)], sink_ref[h] * LOG2E, False))
    outs = _attention(tasks, QK_LOOKAHEAD)
    ob = _pair_tiles(outs[0::2])
    oa = _pair_tiles(outs[1::2])
    o_ref[0] = _finish(oa, ob, za_ref[0], zb_ref[0], ga_ref[0], gb_ref[0], x_ref[0], p_ref[0],
                       woa_ref, wob_ref, wout_ref, gple_ref, wpg_ref, wple_ref)


def _weight_specs(d, ple):
    return [_const_spec((512, d)), _const_spec((512, d)), _const_spec((d, d)),
            _const_spec((1, d)), _const_spec((d, d)), _const_spec((ple, d))]


def _prompt_layer(x, p, proj_args, tat, tbt, sink, weights):
    gin, w_in, gqa, gka, gqb, gkb = proj_args
    bsz, seq, d = x.shape
    tq = ATTN_TILE
    n_hist = BAND // tq
    n_blocks = seq // tq
    cur = lambda w: pl.BlockSpec((1, tq, w), lambda b, i: (b, i, 0))
    nxt = lambda w: pl.BlockSpec((1, tq, w), lambda b, i: (b, jnp.minimum(i + 1, n_blocks - 1), 0))
    tail = lambda r, w: pl.BlockSpec((1, r, w), lambda b, i: (b, 0, 0))
    btail = pl.BlockSpec((1, tq, 512), lambda b, i: (
        b, jnp.clip(i + 1 - (n_blocks - n_hist), 0, n_hist - 1), 0))
    return pl.pallas_call(
        _prompt_kernel,
        grid=(bsz, n_blocks),
        in_specs=[cur(d), nxt(d), cur(p.shape[-1]), _const_spec((1, d)), _const_spec(w_in.shape),
                  _const_spec((1, LANES)), _const_spec((1, LANES)),
                  _const_spec((1, LANES)), _const_spec((1, LANES)),
                  _const_spec(tat.shape), _const_spec(tbt.shape),
                  pl.BlockSpec(memory_space=pltpu.SMEM)]
                 + _weight_specs(d, p.shape[-1]),
        out_specs=[cur(d), tail(WIN, 128), tail(WIN, 128), btail, btail],
        out_shape=[jax.ShapeDtypeStruct((bsz, seq, d), _F32),
                   jax.ShapeDtypeStruct((bsz, WIN, 128), _F32),
                   jax.ShapeDtypeStruct((bsz, WIN, 128), _F32),
                   jax.ShapeDtypeStruct((bsz, BAND, 512), _F32),
                   jax.ShapeDtypeStruct((bsz, BAND, 512), _F32)],
        scratch_shapes=[pltpu.VMEM((2, 512, tq), _BF16), pltpu.VMEM((2, 512, tq), _BF16),
                        pltpu.VMEM((2, tq, _C_END - _C_ZB + _C_QB - _C_ZA), _F32),
                        pltpu.VMEM((n_hist + 2, tq, 512), _BF16), pltpu.VMEM((n_hist + 2, 512, tq), _BF16),
                        pltpu.VMEM((tq, 512), _BF16), pltpu.VMEM((512, tq), _BF16),
                        pltpu.VMEM((WIN + tq, 256), _BF16), pltpu.VMEM((LANES, WIN + tq), _BF16),
                        pltpu.VMEM((tq, 256), _BF16), pltpu.VMEM((LANES, tq), _BF16)],
        compiler_params=pltpu.CompilerParams(
            dimension_semantics=("arbitrary", "arbitrary"),
            vmem_limit_bytes=VMEM_LIMIT_BYTES),
        name="prompt_layer",
    )(x, x, p, gin, w_in, gqa, gka, gqb, gkb, tat, tbt, sink, *weights)


def _sample_attend(acts, caches, x, p, ta, tb, sink, weights):
    qa, ka, va, za, qb, kb, vb, zb, ga, gb = acts
    cka, cva, ckb, cvb = caches
    bsz, seq, d = x.shape
    blk = lambda a: pl.BlockSpec((1,) + a.shape[1:], lambda b: (b, 0, 0))
    ins = (qa, qb, cka, ka, cva, va, ckb, kb, cvb, vb, za, zb, ga, gb, x, p)
    ta_spec = _const_spec((ta.shape[0], seq, ta.shape[2]))
    tb_spec = _const_spec((tb.shape[0], seq, tb.shape[2]))
    return pl.pallas_call(
        _sample_attn_kernel,
        grid=(bsz,),
        in_specs=[blk(a) for a in ins] + [ta_spec, tb_spec, pl.BlockSpec(memory_space=pltpu.SMEM)]
                 + _weight_specs(d, p.shape[-1]),
        out_specs=blk(x),
        out_shape=jax.ShapeDtypeStruct((bsz, seq, d), _F32),
        compiler_params=pltpu.CompilerParams(
            dimension_semantics=("arbitrary",),
            vmem_limit_bytes=VMEM_LIMIT_BYTES),
        name="sample_attn",
    )(*ins, ta, tb, sink, *weights)


def _dup_cache(c):
    return jnp.concatenate([c[:, :, 0], c[:, :, 0], c[:, :, 1], c[:, :, 1]], axis=-1).astype(_BF16)


def kernel(x_prompt, x_sample, cache_k_win, cache_v_win, cache_k_band, cache_v_band, p_prompt, p_sample, g_in, w_in, g_q_win, g_k_win, sink_win, g_q_band, g_k_band, rel_bias_band, w_o_win, w_o_band, w_out, g_ple, w_ple_gate, w_ple):
    depth = w_in.shape[0]
    bsz, seq, d = x_prompt.shape
    dbsz, dseq, _ = x_sample.shape
    assert seq % ATTN_TILE == 0 and seq >= BAND + ATTN_TILE and dseq == CHUNK
    assert ATTN_TILE % WIN_BLOCK == 0 and WIN_BLOCK == WIN and BAND % ATTN_TILE == 0
    assert BAND_HEADS == (WIN_HEADS // 2) * (ATTN_TILE // WIN_BLOCK)
    assert cache_k_win.shape[2] == WIN and cache_k_band.shape[2] == BAND

    h_p, h_s = x_prompt, x_sample
    outs = [[] for _ in range(8)]
    ta, tat = _win_bias_table(WIN_HEADS)
    for i in range(depth):
        pair = lambda g: jnp.tile(g[i], 2)[None, :]
        proj_args = (g_in[i][None, :], w_in[i].astype(_BF16),
                     pair(g_q_win), pair(g_k_win), pair(g_q_band), pair(g_k_band))
        weights = (w_o_win[i].astype(_BF16), w_o_band[i].astype(_BF16), w_out[i].astype(_BF16),
                   g_ple[i][None, :], w_ple_gate[i].astype(_BF16), w_ple[i].astype(_BF16))
        tb, tbt = _band_bias_table(rel_bias_band[i], ATTN_TILE)

        h_p, kat, vat, kbt, vbt = _prompt_layer(h_p, p_prompt[i], proj_args, tat, tbt, sink_win[i], weights)
        outs[0].append(kat.reshape(bsz, WIN, WIN_KV_HEADS, HEAD_DIM))
        outs[1].append(vat.reshape(bsz, WIN, WIN_KV_HEADS, HEAD_DIM))
        outs[2].append(kbt.reshape(bsz, BAND, BAND_HEADS, HEAD_DIM))
        outs[3].append(vbt.reshape(bsz, BAND, BAND_HEADS, HEAD_DIM))

        ntok = dbsz * dseq
        *acts, kan, van, kbn, vbn = _project(h_s.reshape(1, ntok, d), *proj_args,
                                             tile=ntok, tail_a=ntok, tail_b=ntok)
        acts = [a.reshape(dbsz, dseq, a.shape[-1]) for a in acts]
        caches = (_dup_cache(cache_k_win[i]), _dup_cache(cache_v_win[i]),
                  cache_k_band[i].reshape(dbsz, BAND, -1).astype(_BF16),
                  cache_v_band[i].reshape(dbsz, BAND, -1).astype(_BF16))
        h_s = _sample_attend(acts, caches, h_s, p_sample[i], ta, tb, sink_win[i], weights)
        new = lambda t, nh: t.reshape(dbsz, dseq, nh, HEAD_DIM)
        outs[4].append(jnp.concatenate([cache_k_win[i][:, dseq:], new(kan, WIN_KV_HEADS)], axis=1))
        outs[5].append(jnp.concatenate([cache_v_win[i][:, dseq:], new(van, WIN_KV_HEADS)], axis=1))
        outs[6].append(jnp.concatenate([cache_k_band[i][:, dseq:], new(kbn, BAND_HEADS)], axis=1))
        outs[7].append(jnp.concatenate([cache_v_band[i][:, dseq:], new(vbn, BAND_HEADS)], axis=1))
    return (h_p, h_s) + tuple(jnp.stack(o) for o in outs)
```

```python
import functools
import math

import jax
import jax.numpy as jnp
from jax import lax
from jax.experimental import pallas as pl
from jax.experimental.pallas import tpu as pltpu

CHUNK = 64
HEAD_DIM = 64
WIN_HEADS = 8
WIN_KV_HEADS = 2
WIN_CHUNKS = 2
BAND_HEADS = 8
BAND_CHUNKS = 8
REL_CLIP = 128
ALIBI_MAX = 8.0
EPS = 1e-6
NEG_INF = -1e30
LOG2E = math.log2(math.e)

WIN = WIN_CHUNKS * CHUNK
BAND = BAND_CHUNKS * CHUNK
LANES = 128
ATTN_TILE = 256
STEP_BLOCKS = 2
WIN_BLOCK = 128
QK_LOOKAHEAD = 3
FILL_CHUNK = 256
FILL_FRONT = 2
FILL_FINISH = 3
VMEM_LIMIT_BYTES = 56 * 1024 * 1024

_F32 = jnp.float32
_BF16 = jnp.bfloat16
_NT = (((1,), (1,)), ((), ()))


def _rmsnorm(x, g):
    ms = jnp.mean(x * x, axis=-1, keepdims=True)
    return x * lax.rsqrt(ms + EPS) * g


def _lo_lanes(rows=1):
    return lax.broadcasted_iota(jnp.int32, (rows, LANES), 1) < HEAD_DIM


def _head_rmsnorm(u, g_pair):
    lo = _lo_lanes()
    outs = []
    for j in range(u.shape[1] // LANES):
        t = u[:, j * LANES:(j + 1) * LANES]
        sq = t * t
        s_lo = jnp.sum(jnp.where(lo, sq, 0.0), axis=-1, keepdims=True)
        s_hi = jnp.sum(jnp.where(lo, 0.0, sq), axis=-1, keepdims=True)
        ms = jnp.where(lo, s_lo, s_hi) * (1.0 / HEAD_DIM)
        outs.append(t * lax.rsqrt(ms + EPS) * g_pair)
    return outs[0] if len(outs) == 1 else jnp.concatenate(outs, axis=-1)


def _dup_heads(t):
    lo = _lo_lanes()
    r = pltpu.roll(t, HEAD_DIM, axis=1)
    return jnp.concatenate([jnp.where(lo, t, r), jnp.where(lo, r, t)], axis=-1)


_C_QA, _C_KA, _C_VA, _C_ZA, _C_QB, _C_KB, _C_VB, _C_ZB, _C_GA, _C_GB, _C_END = (
    0, 512, 640, 768, 1280, 1792, 2304, 2816, 3328, 4352, 5376)
_Q_SCALE = HEAD_DIM ** -0.5 * LOG2E


def _proj_kernel(x_ref, gin_ref, w_ref, gqa_ref, gka_ref, gqb_ref, gkb_ref,
                 qa_ref, ka_ref, va_ref, za_ref, qb_ref, kb_ref, vb_ref, zb_ref, ga_ref, gb_ref,
                 kat_ref, vat_ref, kbt_ref, vbt_ref):
    last = pl.program_id(1) == pl.num_programs(1) - 1
    xn = _rmsnorm(x_ref[0], gin_ref[...]).astype(_BF16)

    def proj(a, b):
        return jnp.dot(xn, w_ref[:, a:b], preferred_element_type=_F32)

    qa_ref[0] = (_head_rmsnorm(proj(_C_QA, _C_KA), gqa_ref[...]) * _Q_SCALE).astype(_BF16)
    qb_ref[0] = (_head_rmsnorm(proj(_C_QB, _C_KB), gqb_ref[...]) * _Q_SCALE).astype(_BF16)

    ka = _head_rmsnorm(proj(_C_KA, _C_VA), gka_ref[...])
    va = proj(_C_VA, _C_ZA)
    ka_ref[0] = _dup_heads(ka).astype(_BF16)
    va_ref[0] = _dup_heads(va).astype(_BF16)
    kb = _head_rmsnorm(proj(_C_KB, _C_VB), gkb_ref[...])
    vb = proj(_C_VB, _C_ZB)
    kb_ref[0] = kb.astype(_BF16)
    vb_ref[0] = vb.astype(_BF16)

    za_ref[0] = proj(_C_ZA, _C_QB).astype(_BF16)
    zb_ref[0] = proj(_C_ZB, _C_GA).astype(_BF16)
    ga_ref[0] = proj(_C_GA, _C_GB).astype(_BF16)
    gb_ref[0] = proj(_C_GB, _C_END).astype(_BF16)

    @pl.when(last)
    def _():
        ta, tb = kat_ref.shape[1], kbt_ref.shape[1]
        t = ka.shape[0]
        kat_ref[0] = ka[t - ta:, :]
        vat_ref[0] = va[t - ta:, :]
        kbt_ref[0] = kb[t - tb:, :]
        vbt_ref[0] = vb[t - tb:, :]


def _const_spec(shape, index=None):
    index = (0,) * len(shape) if index is None else index
    return pl.BlockSpec(shape, lambda *_: index, pipeline_mode=pl.Buffered(1))


def _project(x, gin, w_in, gqa, gka, gqb, gkb, *, tile, tail_a, tail_b):
    bsz, seq, d = x.shape
    nt = seq // tile
    row = lambda w: pl.BlockSpec((1, tile, w), lambda b, t: (b, t, 0))
    tail = lambda r, w: pl.BlockSpec((1, r, w), lambda b, t: (b, 0, 0))
    act = lambda w: jax.ShapeDtypeStruct((bsz, seq, w), _BF16)
    widths = (512, 256, 256, 512, 512, 512, 512, 512, 1024, 1024)
    return pl.pallas_call(
        _proj_kernel,
        grid=(bsz, nt),
        in_specs=[row(d), _const_spec((1, d)), _const_spec(w_in.shape),
                  _const_spec((1, LANES)), _const_spec((1, LANES)),
                  _const_spec((1, LANES)), _const_spec((1, LANES))],
        out_specs=[row(w) for w in widths] + [tail(tail_a, 128), tail(tail_a, 128),
                                              tail(tail_b, 512), tail(tail_b, 512)],
        out_shape=[act(w) for w in widths] + [
            jax.ShapeDtypeStruct((bsz, tail_a, 128), _F32),
            jax.ShapeDtypeStruct((bsz, tail_a, 128), _F32),
            jax.ShapeDtypeStruct((bsz, tail_b, 512), _F32),
            jax.ShapeDtypeStruct((bsz, tail_b, 512), _F32)],
        compiler_params=pltpu.CompilerParams(
            dimension_semantics=("arbitrary", "arbitrary"),
            vmem_limit_bytes=VMEM_LIMIT_BYTES),
        name="proj",
    )(x, gin, w_in, gqa, gka, gqb, gkb)


def _win_bias_kernel(o_ref, ot_ref):
    heads, rows, width = o_ref.shape
    r = lax.broadcasted_iota(jnp.int32, (rows, width), 0)
    c = lax.broadcasted_iota(jnp.int32, (rows, width), 1)
    dist = jnp.abs(r + WIN - c).astype(_F32)
    qc, kc = r // CHUNK, c // CHUNK
    valid = jnp.logical_and(kc >= qc, kc <= qc + WIN_CHUNKS)
    for h in range(heads):
        slope = 2.0 ** (-ALIBI_MAX * (h + 1) / heads)
        bias = jnp.where(valid, dist * (-slope * LOG2E), NEG_INF)
        o_ref[h] = bias
        ot_ref[h // 2, :, (h % 2) * rows:(h % 2 + 1) * rows] = bias.T


def _win_bias_table(heads):
    return pl.pallas_call(
        _win_bias_kernel,
        out_shape=[jax.ShapeDtypeStruct((heads, WIN_BLOCK, WIN_BLOCK + WIN), _F32),
                   jax.ShapeDtypeStruct((heads // 2, WIN_BLOCK + WIN, 2 * WIN_BLOCK), _F32)],
        name="win_bias",
    )()


def _band_bias_kernel(w_ref, o_ref, ot_ref):
    width, rows = ot_ref.shape[1:]
    t = pltpu.roll(jnp.broadcast_to(w_ref[0], (rows, width)), 0, 1, stride=1, stride_axis=0)
    qc = lax.broadcasted_iota(jnp.int32, (rows, width), 0) // CHUNK
    kc = lax.broadcasted_iota(jnp.int32, (rows, width), 1) // CHUNK
    valid = jnp.logical_and(kc >= qc, kc <= qc + BAND_CHUNKS)
    bias = jnp.where(valid, t * LOG2E, NEG_INF)
    o_ref[0] = bias[:o_ref.shape[1]]
    ot_ref[0] = bias.T


def _band_bias_table(table, rows):
    heads = table.shape[0]
    width = rows + BAND
    assert rows >= 2 * CHUNK and rows % CHUNK == 0
    far = table[:, 2 * REL_CLIP:]
    w = jnp.concatenate([jnp.broadcast_to(far, (heads, BAND - REL_CLIP)), table[:, ::-1],
                         jnp.broadcast_to(far, (heads, width - BAND - REL_CLIP - 1))], axis=1)
    return pl.pallas_call(
        _band_bias_kernel,
        grid=(heads,),
        in_specs=[pl.BlockSpec((1, 1, width), lambda h: (h, 0, 0))],
        out_specs=[pl.BlockSpec((1, CHUNK, width), lambda h: (h, 0, 0)),
                   pl.BlockSpec((1, width, rows), lambda h: (h, 0, 0))],
        out_shape=[jax.ShapeDtypeStruct((heads, CHUNK, width), _F32),
                   jax.ShapeDtypeStruct((heads, width, rows), _F32)],
        name="band_bias",
    )(w.astype(_F32)[:, None, :])


def _reduce_keys(pieces, elementwise, reduce, axis):
    by_size = {}
    for x in pieces:
        n = x.shape[axis]
        by_size[n] = x if n not in by_size else elementwise(by_size[n], x)
    return functools.reduce(elementwise, [reduce(x, axis=axis, keepdims=True)
                                          for x in by_size.values()])


def _attention(tasks, lookahead, fillers=(), front=0):
    def scores(task):
        q, ks, _, bs, _, keys_on_rows = task
        if keys_on_rows:
            return [jnp.dot(k(), q, preferred_element_type=_F32) + b() for k, b in zip(ks, bs)]
        return [lax.dot_general(q, k(), _NT, preferred_element_type=_F32) + b()
                for k, b in zip(ks, bs)]

    pending = [scores(t) for t in tasks[:lookahead]]
    outs = []
    filled = 0
    for idx, (_, _, vs, _, sink, keys_on_rows) in enumerate(tasks):
        if idx + lookahead < len(tasks):
            pending.append(scores(tasks[idx + lookahead]))
        due = front + -(-(idx + 1) * (len(fillers) - front) // len(tasks))
        while filled < min(due, len(fillers)):
            fillers[filled]()
            filled += 1
        s = pending[idx]
        pending[idx] = None
        axis = 0 if keys_on_rows else 1
        m = _reduce_keys(s, jnp.maximum, jnp.max, axis)
        if sink is not None:
            m = jnp.maximum(m, sink)
        e = [jnp.exp2(sj - m) for sj in s]
        den = _reduce_keys(e, jnp.add, jnp.sum, axis)
        if sink is not None:
            den = den + jnp.exp2(sink - m)
        if keys_on_rows:
            pv = [jnp.dot(v(), ej.astype(_BF16), preferred_element_type=_F32) for ej, v in zip(e, vs)]
        else:
            pv = [jnp.dot(ej.astype(_BF16), v(), preferred_element_type=_F32) for ej, v in zip(e, vs)]
        outs.append(functools.reduce(jnp.add, pv) / den)
    return outs


def _masked_q(qp, half):
    keep = _lo_lanes() if half == 0 else jnp.logical_not(_lo_lanes())
    return jnp.where(keep, qp, jnp.zeros_like(qp))


def _masked_rows(qtp, half):
    lo = lax.broadcasted_iota(jnp.int32, qtp.shape, 0) < HEAD_DIM
    return jnp.where(lo if half == 0 else jnp.logical_not(lo), qtp, jnp.zeros_like(qtp))


def _pair_cols(q, h):
    return q[:, (h // 2) * LANES:(h // 2 + 1) * LANES]


def _pair_tiles(per_head):
    lo = _lo_lanes()
    return jnp.concatenate([jnp.where(lo, per_head[2 * p], per_head[2 * p + 1])
                            for p in range(len(per_head) // 2)], axis=-1)


def _finish(oa, ob, za, zb, ga, gb, x, p, woa_ref, wob_ref, wout_ref, gple_ref, wpg_ref, wple_ref,
            fillers=()):
    fillers = list(fillers)
    za = za.astype(_F32)
    zb = zb.astype(_F32)
    br_a = jnp.dot((oa * jax.nn.silu(za)).astype(_BF16), woa_ref[...], preferred_element_type=_F32)
    br_b = jnp.dot((ob * jax.nn.silu(zb)).astype(_BF16), wob_ref[...], preferred_element_type=_F32)
    mrg = jax.nn.sigmoid(ga.astype(_F32)) * br_a + jax.nn.sigmoid(gb.astype(_F32)) * br_b
    h2 = x + jnp.dot(mrg.astype(_BF16), wout_ref[...], preferred_element_type=_F32)
    ple = jnp.dot(p.astype(_BF16), wple_ref[...], preferred_element_type=_F32)
    for f in fillers[:len(fillers) // 2]:
        f()
    gate = jax.nn.sigmoid(jnp.dot(_rmsnorm(h2, gple_ref[...]).astype(_BF16), wpg_ref[...],
                                  preferred_element_type=_F32))
    for f in fillers[len(fillers) // 2:]:
        f()
    return h2 + gate * ple


def _win_col(h):
    return LANES * (h // (WIN_HEADS // WIN_KV_HEADS))


def _band_col(h):
    return LANES * (h // 2)


def _prompt_kernel(xc_ref, xn_ref, p_ref, gin_ref, w_ref, gqa_ref, gka_ref, gqb_ref, gkb_ref,
                   tat_ref, tbt_ref, sink_ref,
                   woa_ref, wob_ref, wout_ref, gple_ref, wpg_ref, wple_ref,
                   o_ref, kat_ref, vat_ref, kbt_ref, vbt_ref,
                   qat_ref, qbt_ref, zg_ref, kbr_ref, vbr_ref, kbn_ref, vbn_ref,
                   kaw_ref, vaw_ref, kan_ref, van_ref):
    step = pl.program_id(1)
    rows = ATTN_TILE
    n_blk = rows // WIN_BLOCK
    ring = kbr_ref.shape[0]
    n_hist = BAND // rows

    def projection(x_tile, slot, kb_dst, vb_dst, ka_dst, va_dst):
        cache = {}

        def proj(a, b):
            if not cache:
                cache["xn"] = _rmsnorm(x_tile(), gin_ref[...]).astype(_BF16)
            return jnp.dot(cache["xn"], w_ref[:, a:b], preferred_element_type=_F32)

        def qa_cols(a):
            q = _head_rmsnorm(proj(a, a + FILL_CHUNK), gqa_ref[...]) * _Q_SCALE
            qat_ref[slot, a - _C_QA:a - _C_QA + FILL_CHUNK, :] = q.T.astype(_BF16)

        def qb_cols(a):
            q = _head_rmsnorm(proj(a, a + FILL_CHUNK), gqb_ref[...]) * _Q_SCALE
            qbt_ref[slot, a - _C_QB:a - _C_QB + FILL_CHUNK, :] = q.T.astype(_BF16)

        def kb_cols(a):
            k = _head_rmsnorm(proj(a, a + FILL_CHUNK), gkb_ref[...])
            kb_dst[:, a - _C_KB:a - _C_KB + FILL_CHUNK] = k.astype(_BF16)
            kbt_ref[0, :, a - _C_KB:a - _C_KB + FILL_CHUNK] = k

        def vb_cols(a):
            v = proj(a, a + FILL_CHUNK)
            vb_dst[a - _C_VB:a - _C_VB + FILL_CHUNK, :] = v.T.astype(_BF16)
            vbt_ref[0, :, a - _C_VB:a - _C_VB + FILL_CHUNK] = v

        def kva_cols():
            kva = proj(_C_KA, _C_ZA)
            ka_new = _head_rmsnorm(kva[:, :LANES], gka_ref[...])
            va_new = kva[:, LANES:]
            ka_dst[...] = _dup_heads(ka_new).astype(_BF16)
            va_dst[...] = va_new.T.astype(_BF16)
            kat_ref[0] = ka_new[rows - WIN:]
            vat_ref[0] = va_new[rows - WIN:]

        def gate_cols(a, dst):
            zg_ref[slot, :, dst:dst + FILL_CHUNK] = proj(a, a + FILL_CHUNK).astype(zg_ref.dtype)

        part = functools.partial
        return ([part(qb_cols, a) for a in range(_C_QB, _C_KB, FILL_CHUNK)]
                + [part(kb_cols, a) for a in range(_C_KB, _C_VB, FILL_CHUNK)]
                + [part(qa_cols, a) for a in range(_C_QA, _C_KA, FILL_CHUNK)]
                + [kva_cols]
                + [part(vb_cols, a) for a in range(_C_VB, _C_ZB, FILL_CHUNK)]
                + [part(gate_cols, a, a - _C_ZA) for a in range(_C_ZA, _C_QB, FILL_CHUNK)]
                + [part(gate_cols, a, a - _C_ZB + 512) for a in range(_C_ZB, _C_END, FILL_CHUNK)])

    @pl.when(step == 0)
    def _():
        for j in range(1, n_hist + 1):
            kbr_ref[ring - j] = jnp.zeros(kbr_ref.shape[1:], _BF16)
            vbr_ref[ring - j] = jnp.zeros(vbr_ref.shape[1:], _BF16)
        kaw_ref[0:WIN] = jnp.zeros((WIN, kaw_ref.shape[1]), _BF16)
        vaw_ref[:, 0:WIN] = jnp.zeros((vaw_ref.shape[0], WIN), _BF16)
        for f in projection(lambda: xc_ref[0, 0:rows], 0, kbr_ref.at[0], vbr_ref.at[0],
                            kaw_ref.at[WIN:], vaw_ref.at[:, WIN:]):
            f()

    heads_per_kv = WIN_HEADS // WIN_KV_HEADS
    lane = lax.broadcasted_iota(jnp.int32, (1, 2 * WIN_BLOCK), 1)

    def attend_block(sb):
        i = step * STEP_BLOCKS + sb
        cur = sb % 2
        lo, hi = sb * rows, (sb + 1) * rows
        if sb + 1 < STEP_BLOCKS:
            x_next = lambda: xc_ref[0, hi:hi + rows]
        else:
            x_next = lambda: xn_ref[0]
        fillers = projection(x_next, 1 - cur, kbn_ref, vbn_ref, kan_ref, van_ref)

        def band_bias(h, j):
            b = tbt_ref[h, j * rows:(j + 1) * rows, :]
            return b if j == n_hist else jnp.where(i >= n_hist - j, b, NEG_INF)

        def win_bias(pair, j):
            b = tat_ref[pair]
            if j > 0:
                return b
            return jnp.concatenate([jnp.where(i >= 1, b[:WIN], NEG_INF), b[WIN:]], axis=0)

        n_late = FILL_FINISH
        slots = [(i + ring - n_hist + j) % ring for j in range(n_hist + 1)]
        tasks = []
        for h in range(BAND_HEADS):
            c = _band_col(h)
            tasks.append((
                _masked_rows(qbt_ref[cur, c:c + LANES, :], h % 2),
                [functools.partial(lambda s, c: kbr_ref[s, :, c:c + LANES], s, c) for s in slots],
                [functools.partial(lambda s, h: vbr_ref[s, h * HEAD_DIM:(h + 1) * HEAD_DIM, :], s, h)
                 for s in slots],
                [functools.partial(band_bias, h, j) for j in range(n_hist + 1)],
                None, True))
            pair, j = h // n_blk, h % n_blk
            kv = 2 * pair // heads_per_kv
            r0 = j * WIN_BLOCK
            qtp = qat_ref[cur, pair * LANES:(pair + 1) * LANES, r0:r0 + WIN_BLOCK]
            tasks.append((
                jnp.concatenate([_masked_rows(qtp, 0), _masked_rows(qtp, 1)], axis=1),
                [functools.partial(
                    lambda r0, kv: kaw_ref[r0:r0 + WIN_BLOCK + WIN, kv * LANES:(kv + 1) * LANES], r0, kv)],
                [functools.partial(
                    lambda r0, kv: vaw_ref[kv * HEAD_DIM:(kv + 1) * HEAD_DIM, r0:r0 + WIN_BLOCK + WIN], r0, kv)],
                [functools.partial(win_bias, pair, j)],
                jnp.where(lane < WIN_BLOCK, sink_ref[2 * pair], sink_ref[2 * pair + 1]) * LOG2E, True))
        outs = _attention(tasks, QK_LOOKAHEAD, fillers[:len(fillers) - n_late], FILL_FRONT)
        ob = jnp.concatenate(outs[0::2], axis=0).T
        wouts = outs[1::2]
        oa = jnp.concatenate(
            [jnp.concatenate([wouts[(h // 2) * n_blk + j][:, (h % 2) * WIN_BLOCK:(h % 2 + 1) * WIN_BLOCK]
                              for j in range(n_blk)], axis=1) for h in range(WIN_HEADS)], axis=0).T
        o_ref[0, lo:hi] = _finish(
            oa, ob, zg_ref[cur, :, 0:512], zg_ref[cur, :, 512:1024], zg_ref[cur, :, 1024:2048],
            zg_ref[cur, :, 2048:3072], xc_ref[0, lo:hi], p_ref[0, lo:hi],
            woa_ref, wob_ref, wout_ref, gple_ref, wpg_ref, wple_ref, fillers[len(fillers) - n_late:])

        kbr_ref[(i + 1) % ring] = kbn_ref[...]
        vbr_ref[(i + 1) % ring] = vbn_ref[...]
        kaw_ref[0:WIN] = kaw_ref[rows:rows + WIN]
        vaw_ref[:, 0:WIN] = vaw_ref[:, rows:rows + WIN]
        kaw_ref[WIN:] = kan_ref[...]
        vaw_ref[:, WIN:] = van_ref[...]

    for sb in range(STEP_BLOCKS):
        attend_block(sb)


def _sample_attn_kernel(qa_ref, qb_ref, kac_ref, kan_ref, vac_ref, van_ref,
                        kbc_ref, kbn_ref, vbc_ref, vbn_ref,
                        za_ref, zb_ref, ga_ref, gb_ref, x_ref, p_ref, ta_ref, tb_ref, sink_ref,
                        woa_ref, wob_ref, wout_ref, gple_ref, wpg_ref, wple_ref, o_ref):
    rows = qa_ref.shape[1]
    half = kbc_ref.shape[1] // 2
    qa, qb = qa_ref[0], qb_ref[0]
    a_edges = (0, kac_ref.shape[1], kac_ref.shape[1] + rows)
    b_edges = (0, half, 2 * half, 2 * half + rows)

    def pieces(cache_ref, new_ref, c, splits):
        n = cache_ref.shape[1] // splits
        return ([functools.partial(lambda j: cache_ref[0, j * n:(j + 1) * n, c:c + LANES], j)
                 for j in range(splits)] + [lambda: new_ref[0, :, c:c + LANES]])

    tasks = []
    for h in range(BAND_HEADS):
        c = _band_col(h)
        tasks.append((_masked_q(_pair_cols(qb, h), h % 2), pieces(kbc_ref, kbn_ref, c, 2), pieces(vbc_ref, vbn_ref, c, 2),
                      [functools.partial(lambda h, j: tb_ref[h, :, b_edges[j]:b_edges[j + 1]], h, j)
                       for j in range(3)], None, False))
        c = _win_col(h)
        tasks.append((_masked_q(_pair_cols(qa, h), h % 2), pieces(kac_ref, kan_ref, c, 1), pieces(vac_ref, van_ref, c, 1),
                      [functools.partial(lambda h, j: ta_ref[h, :, a_edges[j]:a_edges[j + 1]], h, j)
                       for j in range(2)], sink_ref[h] * LOG2E, False))
    outs = _attention(tasks, QK_LOOKAHEAD)
    ob = _pair_tiles(outs[0::2])
    oa = _pair_tiles(outs[1::2])
    o_ref[0] = _finish(oa, ob, za_ref[0], zb_ref[0], ga_ref[0], gb_ref[0], x_ref[0], p_ref[0],
                       woa_ref, wob_ref, wout_ref, gple_ref, wpg_ref, wple_ref)


def _weight_specs(d, ple):
    return [_const_spec((512, d)), _const_spec((512, d)), _const_spec((d, d)),
            _const_spec((1, d)), _const_spec((d, d)), _const_spec((ple, d))]


def _prompt_layer(x, p, proj_args, tat, tbt, sink, weights):
    gin, w_in, gqa, gka, gqb, gkb = proj_args
    bsz, seq, d = x.shape
    tq = ATTN_TILE
    n_hist = BAND // tq
    n_blocks = seq // tq
    cur = lambda w: pl.BlockSpec((1, STEP_BLOCKS * tq, w), lambda b, i: (b, i, 0))
    nxt = lambda w: pl.BlockSpec((1, tq, w), lambda b, i: (
        b, jnp.minimum(STEP_BLOCKS * (i + 1), n_blocks - 1), 0))
    tail = lambda r, w: pl.BlockSpec((1, r, w), lambda b, i: (b, 0, 0))
    assert (n_blocks - n_hist) % STEP_BLOCKS == 0
    btail = pl.BlockSpec((1, tq, 512), lambda b, i: (
        b, jnp.clip(STEP_BLOCKS * i + 1 - (n_blocks - n_hist), 0, n_hist - 1), 0))
    return pl.pallas_call(
        _prompt_kernel,
        grid=(bsz, n_blocks // STEP_BLOCKS),
        in_specs=[cur(d), nxt(d), cur(p.shape[-1]), _const_spec((1, d)), _const_spec(w_in.shape),
                  _const_spec((1, LANES)), _const_spec((1, LANES)),
                  _const_spec((1, LANES)), _const_spec((1, LANES)),
                  _const_spec(tat.shape), _const_spec(tbt.shape),
                  pl.BlockSpec(memory_space=pltpu.SMEM)]
                 + _weight_specs(d, p.shape[-1]),
        out_specs=[cur(d), tail(WIN, 128), tail(WIN, 128), btail, btail],
        out_shape=[jax.ShapeDtypeStruct((bsz, seq, d), _F32),
                   jax.ShapeDtypeStruct((bsz, WIN, 128), _F32),
                   jax.ShapeDtypeStruct((bsz, WIN, 128), _F32),
                   jax.ShapeDtypeStruct((bsz, BAND, 512), _F32),
                   jax.ShapeDtypeStruct((bsz, BAND, 512), _F32)],
        scratch_shapes=[pltpu.VMEM((2, 512, tq), _BF16), pltpu.VMEM((2, 512, tq), _BF16),
                        pltpu.VMEM((2, tq, _C_END - _C_ZB + _C_QB - _C_ZA), _BF16),
                        pltpu.VMEM((n_hist + 2, tq, 512), _BF16), pltpu.VMEM((n_hist + 2, 512, tq), _BF16),
                        pltpu.VMEM((tq, 512), _BF16), pltpu.VMEM((512, tq), _BF16),
                        pltpu.VMEM((WIN + tq, 256), _BF16), pltpu.VMEM((LANES, WIN + tq), _BF16),
                        pltpu.VMEM((tq, 256), _BF16), pltpu.VMEM((LANES, tq), _BF16)],
        compiler_params=pltpu.CompilerParams(
            dimension_semantics=("arbitrary", "arbitrary"),
            vmem_limit_bytes=VMEM_LIMIT_BYTES),
        name="prompt_layer",
    )(x, x, p, gin, w_in, gqa, gka, gqb, gkb, tat, tbt, sink, *weights)


def _sample_attend(acts, caches, x, p, ta, tb, sink, weights):
    qa, ka, va, za, qb, kb, vb, zb, ga, gb = acts
    cka, cva, ckb, cvb = caches
    bsz, seq, d = x.shape
    blk = lambda a: pl.BlockSpec((1,) + a.shape[1:], lambda b: (b, 0, 0))
    ins = (qa, qb, cka, ka, cva, va, ckb, kb, cvb, vb, za, zb, ga, gb, x, p)
    ta_spec = _const_spec((ta.shape[0], seq, ta.shape[2]))
    tb_spec = _const_spec((tb.shape[0], seq, tb.shape[2]))
    return pl.pallas_call(
        _sample_attn_kernel,
        grid=(bsz,),
        in_specs=[blk(a) for a in ins] + [ta_spec, tb_spec, pl.BlockSpec(memory_space=pltpu.SMEM)]
                 + _weight_specs(d, p.shape[-1]),
        out_specs=blk(x),
        out_shape=jax.ShapeDtypeStruct((bsz, seq, d), _F32),
        compiler_params=pltpu.CompilerParams(
            dimension_semantics=("arbitrary",),
            vmem_limit_bytes=VMEM_LIMIT_BYTES),
        name="sample_attn",
    )(*ins, ta, tb, sink, *weights)


def _dup_cache(c):
    return jnp.concatenate([c[:, :, 0], c[:, :, 0], c[:, :, 1], c[:, :, 1]], axis=-1).astype(_BF16)


def kernel(x_prompt, x_sample, cache_k_win, cache_v_win, cache_k_band, cache_v_band, p_prompt, p_sample, g_in, w_in, g_q_win, g_k_win, sink_win, g_q_band, g_k_band, rel_bias_band, w_o_win, w_o_band, w_out, g_ple, w_ple_gate, w_ple):
    depth = w_in.shape[0]
    bsz, seq, d = x_prompt.shape
    dbsz, dseq, _ = x_sample.shape
    assert seq % (STEP_BLOCKS * ATTN_TILE) == 0 and seq >= BAND + ATTN_TILE and dseq == CHUNK
    assert STEP_BLOCKS % 2 == 0
    assert ATTN_TILE % WIN_BLOCK == 0 and WIN_BLOCK == WIN and BAND % ATTN_TILE == 0
    assert BAND_HEADS == (WIN_HEADS // 2) * (ATTN_TILE // WIN_BLOCK)
    assert cache_k_win.shape[2] == WIN and cache_k_band.shape[2] == BAND

    h_p, h_s = x_prompt, x_sample
    outs = [[] for _ in range(8)]
    ta, tat = _win_bias_table(WIN_HEADS)
    for i in range(depth):
        pair = lambda g: jnp.tile(g[i], 2)[None, :]
        proj_args = (g_in[i][None, :], w_in[i].astype(_BF16),
                     pair(g_q_win), pair(g_k_win), pair(g_q_band), pair(g_k_band))
        weights = (w_o_win[i].astype(_BF16), w_o_band[i].astype(_BF16), w_out[i].astype(_BF16),
                   g_ple[i][None, :], w_ple_gate[i].astype(_BF16), w_ple[i].astype(_BF16))
        tb, tbt = _band_bias_table(rel_bias_band[i], ATTN_TILE)

        h_p, kat, vat, kbt, vbt = _prompt_layer(h_p, p_prompt[i], proj_args, tat, tbt, sink_win[i], weights)
        outs[0].append(kat.reshape(bsz, WIN, WIN_KV_HEADS, HEAD_DIM))
        outs[1].append(vat.reshape(bsz, WIN, WIN_KV_HEADS, HEAD_DIM))
        outs[2].append(kbt.reshape(bsz, BAND, BAND_HEADS, HEAD_DIM))
        outs[3].append(vbt.reshape(bsz, BAND, BAND_HEADS, HEAD_DIM))

        ntok = dbsz * dseq
        *acts, kan, van, kbn, vbn = _project(h_s.reshape(1, ntok, d), *proj_args,
                                             tile=ntok, tail_a=ntok, tail_b=ntok)
        acts = [a.reshape(dbsz, dseq, a.shape[-1]) for a in acts]
        caches = (_dup_cache(cache_k_win[i]), _dup_cache(cache_v_win[i]),
                  cache_k_band[i].reshape(dbsz, BAND, -1).astype(_BF16),
                  cache_v_band[i].reshape(dbsz, BAND, -1).astype(_BF16))
        h_s = _sample_attend(acts, caches, h_s, p_sample[i], ta, tb, sink_win[i], weights)
        new = lambda t, nh: t.reshape(dbsz, dseq, nh, HEAD_DIM)
        outs[4].append(jnp.concatenate([cache_k_win[i][:, dseq:], new(kan, WIN_KV_HEADS)], axis=1))
        outs[5].append(jnp.concatenate([cache_v_win[i][:, dseq:], new(van, WIN_KV_HEADS)], axis=1))
        outs[6].append(jnp.concatenate([cache_k_band[i][:, dseq:], new(kbn, BAND_HEADS)], axis=1))
        outs[7].append(jnp.concatenate([cache_v_band[i][:, dseq:], new(vbn, BAND_HEADS)], axis=1))
    return (h_p, h_s) + tuple(jnp.stack(o) for o in outs)
```

```python
import functools
import math

import jax
import jax.numpy as jnp
from jax import lax
from jax.experimental import pallas as pl
from jax.experimental.pallas import tpu as pltpu

CHUNK = 64
HEAD_DIM = 64
WIN_HEADS = 8
WIN_KV_HEADS = 2
WIN_CHUNKS = 2
BAND_HEADS = 8
BAND_CHUNKS = 8
REL_CLIP = 128
ALIBI_MAX = 8.0
EPS = 1e-6
NEG_INF = -1e30
LOG2E = math.log2(math.e)

WIN = WIN_CHUNKS * CHUNK
BAND = BAND_CHUNKS * CHUNK
LANES = 128
ATTN_TILE = 256
STEP_BLOCKS = 2
SAMPLE_SEQS = 4
WIN_BLOCK = 128
QK_LOOKAHEAD = 3
FILL_CHUNK = 256
FILL_FRONT = 2
FILL_FINISH = 3
VMEM_LIMIT_BYTES = 56 * 1024 * 1024

_F32 = jnp.float32
_BF16 = jnp.bfloat16
_NT = (((1,), (1,)), ((), ()))


def _rmsnorm(x, g):
    ms = jnp.mean(x * x, axis=-1, keepdims=True)
    return x * lax.rsqrt(ms + EPS) * g


def _lo_lanes(rows=1):
    return lax.broadcasted_iota(jnp.int32, (rows, LANES), 1) < HEAD_DIM


def _head_rmsnorm(u, g_pair):
    lo = _lo_lanes()
    outs = []
    for j in range(u.shape[1] // LANES):
        t = u[:, j * LANES:(j + 1) * LANES]
        sq = t * t
        s_lo = jnp.sum(jnp.where(lo, sq, 0.0), axis=-1, keepdims=True)
        s_hi = jnp.sum(jnp.where(lo, 0.0, sq), axis=-1, keepdims=True)
        ms = jnp.where(lo, s_lo, s_hi) * (1.0 / HEAD_DIM)
        outs.append(t * lax.rsqrt(ms + EPS) * g_pair)
    return outs[0] if len(outs) == 1 else jnp.concatenate(outs, axis=-1)


def _dup_heads(t):
    lo = _lo_lanes()
    r = pltpu.roll(t, HEAD_DIM, axis=1)
    return jnp.concatenate([jnp.where(lo, t, r), jnp.where(lo, r, t)], axis=-1)


_C_QA, _C_KA, _C_VA, _C_ZA, _C_QB, _C_KB, _C_VB, _C_ZB, _C_GA, _C_GB, _C_END = (
    0, 512, 640, 768, 1280, 1792, 2304, 2816, 3328, 4352, 5376)
_Q_SCALE = HEAD_DIM ** -0.5 * LOG2E


def _proj_kernel(x_ref, gin_ref, w_ref, gqa_ref, gka_ref, gqb_ref, gkb_ref,
                 qa_ref, ka_ref, va_ref, za_ref, qb_ref, kb_ref, vb_ref, zb_ref, ga_ref, gb_ref,
                 kat_ref, vat_ref, kbt_ref, vbt_ref):
    last = pl.program_id(1) == pl.num_programs(1) - 1
    xn = _rmsnorm(x_ref[0], gin_ref[...]).astype(_BF16)

    def proj(a, b):
        return jnp.dot(xn, w_ref[:, a:b], preferred_element_type=_F32)

    qa_ref[0] = (_head_rmsnorm(proj(_C_QA, _C_KA), gqa_ref[...]) * _Q_SCALE).astype(_BF16)
    qb_ref[0] = (_head_rmsnorm(proj(_C_QB, _C_KB), gqb_ref[...]) * _Q_SCALE).astype(_BF16)

    ka = _head_rmsnorm(proj(_C_KA, _C_VA), gka_ref[...])
    va = proj(_C_VA, _C_ZA)
    ka_ref[0] = _dup_heads(ka).astype(_BF16)
    va_ref[0] = _dup_heads(va).astype(_BF16)
    kb = _head_rmsnorm(proj(_C_KB, _C_VB), gkb_ref[...])
    vb = proj(_C_VB, _C_ZB)
    kb_ref[0] = kb.astype(_BF16)
    vb_ref[0] = vb.astype(_BF16)

    za_ref[0] = proj(_C_ZA, _C_QB).astype(_BF16)
    zb_ref[0] = proj(_C_ZB, _C_GA).astype(_BF16)
    ga_ref[0] = proj(_C_GA, _C_GB).astype(_BF16)
    gb_ref[0] = proj(_C_GB, _C_END).astype(_BF16)

    @pl.when(last)
    def _():
        ta, tb = kat_ref.shape[1], kbt_ref.shape[1]
        t = ka.shape[0]
        kat_ref[0] = ka[t - ta:, :]
        vat_ref[0] = va[t - ta:, :]
        kbt_ref[0] = kb[t - tb:, :]
        vbt_ref[0] = vb[t - tb:, :]


def _const_spec(shape, index=None):
    index = (0,) * len(shape) if index is None else index
    return pl.BlockSpec(shape, lambda *_: index, pipeline_mode=pl.Buffered(1))


def _project(x, gin, w_in, gqa, gka, gqb, gkb, *, tile, tail_a, tail_b):
    bsz, seq, d = x.shape
    nt = seq // tile
    row = lambda w: pl.BlockSpec((1, tile, w), lambda b, t: (b, t, 0))
    tail = lambda r, w: pl.BlockSpec((1, r, w), lambda b, t: (b, 0, 0))
    act = lambda w: jax.ShapeDtypeStruct((bsz, seq, w), _BF16)
    widths = (512, 256, 256, 512, 512, 512, 512, 512, 1024, 1024)
    return pl.pallas_call(
        _proj_kernel,
        grid=(bsz, nt),
        in_specs=[row(d), _const_spec((1, d)), _const_spec(w_in.shape),
                  _const_spec((1, LANES)), _const_spec((1, LANES)),
                  _const_spec((1, LANES)), _const_spec((1, LANES))],
        out_specs=[row(w) for w in widths] + [tail(tail_a, 128), tail(tail_a, 128),
                                              tail(tail_b, 512), tail(tail_b, 512)],
        out_shape=[act(w) for w in widths] + [
            jax.ShapeDtypeStruct((bsz, tail_a, 128), _F32),
            jax.ShapeDtypeStruct((bsz, tail_a, 128), _F32),
            jax.ShapeDtypeStruct((bsz, tail_b, 512), _F32),
            jax.ShapeDtypeStruct((bsz, tail_b, 512), _F32)],
        compiler_params=pltpu.CompilerParams(
            dimension_semantics=("arbitrary", "arbitrary"),
            vmem_limit_bytes=VMEM_LIMIT_BYTES),
        name="proj",
    )(x, gin, w_in, gqa, gka, gqb, gkb)


def _win_bias_kernel(o_ref, ot_ref):
    heads, rows, width = o_ref.shape
    r = lax.broadcasted_iota(jnp.int32, (rows, width), 0)
    c = lax.broadcasted_iota(jnp.int32, (rows, width), 1)
    dist = jnp.abs(r + WIN - c).astype(_F32)
    qc, kc = r // CHUNK, c // CHUNK
    valid = jnp.logical_and(kc >= qc, kc <= qc + WIN_CHUNKS)
    for h in range(heads):
        slope = 2.0 ** (-ALIBI_MAX * (h + 1) / heads)
        bias = jnp.where(valid, dist * (-slope * LOG2E), NEG_INF)
        o_ref[h] = bias
        ot_ref[h // 2, :, (h % 2) * rows:(h % 2 + 1) * rows] = bias.T


def _win_bias_table(heads):
    return pl.pallas_call(
        _win_bias_kernel,
        out_shape=[jax.ShapeDtypeStruct((heads, WIN_BLOCK, WIN_BLOCK + WIN), _F32),
                   jax.ShapeDtypeStruct((heads // 2, WIN_BLOCK + WIN, 2 * WIN_BLOCK), _F32)],
        name="win_bias",
    )()


def _band_bias_kernel(w_ref, o_ref, ot_ref):
    width, rows = ot_ref.shape[1:]
    t = pltpu.roll(jnp.broadcast_to(w_ref[0], (rows, width)), 0, 1, stride=1, stride_axis=0)
    qc = lax.broadcasted_iota(jnp.int32, (rows, width), 0) // CHUNK
    kc = lax.broadcasted_iota(jnp.int32, (rows, width), 1) // CHUNK
    valid = jnp.logical_and(kc >= qc, kc <= qc + BAND_CHUNKS)
    bias = jnp.where(valid, t * LOG2E, NEG_INF)
    o_ref[0] = bias[:o_ref.shape[1]]
    ot_ref[0] = bias.T


def _band_bias_table(table, rows):
    heads = table.shape[0]
    width = rows + BAND
    assert rows >= 2 * CHUNK and rows % CHUNK == 0
    far = table[:, 2 * REL_CLIP:]
    w = jnp.concatenate([jnp.broadcast_to(far, (heads, BAND - REL_CLIP)), table[:, ::-1],
                         jnp.broadcast_to(far, (heads, width - BAND - REL_CLIP - 1))], axis=1)
    return pl.pallas_call(
        _band_bias_kernel,
        grid=(heads,),
        in_specs=[pl.BlockSpec((1, 1, width), lambda h: (h, 0, 0))],
        out_specs=[pl.BlockSpec((1, CHUNK, width), lambda h: (h, 0, 0)),
                   pl.BlockSpec((1, width, rows), lambda h: (h, 0, 0))],
        out_shape=[jax.ShapeDtypeStruct((heads, CHUNK, width), _F32),
                   jax.ShapeDtypeStruct((heads, width, rows), _F32)],
        name="band_bias",
    )(w.astype(_F32)[:, None, :])


def _reduce_keys(pieces, elementwise, reduce, axis):
    by_size = {}
    for x in pieces:
        n = x.shape[axis]
        by_size[n] = x if n not in by_size else elementwise(by_size[n], x)
    return functools.reduce(elementwise, [reduce(x, axis=axis, keepdims=True)
                                          for x in by_size.values()])


def _attention(tasks, lookahead, fillers=(), front=0):
    def scores(task):
        q, ks, _, bs, _, keys_on_rows = task
        if keys_on_rows:
            return [jnp.dot(k(), q, preferred_element_type=_F32) + b() for k, b in zip(ks, bs)]
        return [lax.dot_general(q, k(), _NT, preferred_element_type=_F32) + b()
                for k, b in zip(ks, bs)]

    pending = [scores(t) for t in tasks[:lookahead]]
    outs = []
    filled = 0
    for idx, (_, _, vs, _, sink, keys_on_rows) in enumerate(tasks):
        if idx + lookahead < len(tasks):
            pending.append(scores(tasks[idx + lookahead]))
        due = front + -(-(idx + 1) * (len(fillers) - front) // len(tasks))
        while filled < min(due, len(fillers)):
            fillers[filled]()
            filled += 1
        s = pending[idx]
        pending[idx] = None
        axis = 0 if keys_on_rows else 1
        m = _reduce_keys(s, jnp.maximum, jnp.max, axis)
        if sink is not None:
            m = jnp.maximum(m, sink)
        e = [jnp.exp2(sj - m) for sj in s]
        den = _reduce_keys(e, jnp.add, jnp.sum, axis)
        if sink is not None:
            den = den + jnp.exp2(sink - m)
        if keys_on_rows:
            pv = [jnp.dot(v(), ej.astype(_BF16), preferred_element_type=_F32) for ej, v in zip(e, vs)]
        else:
            pv = [jnp.dot(ej.astype(_BF16), v(), preferred_element_type=_F32) for ej, v in zip(e, vs)]
        outs.append(functools.reduce(jnp.add, pv) / den)
    return outs


def _masked_q(qp, half):
    keep = _lo_lanes() if half == 0 else jnp.logical_not(_lo_lanes())
    return jnp.where(keep, qp, jnp.zeros_like(qp))


def _masked_rows(qtp, half):
    lo = lax.broadcasted_iota(jnp.int32, qtp.shape, 0) < HEAD_DIM
    return jnp.where(lo if half == 0 else jnp.logical_not(lo), qtp, jnp.zeros_like(qtp))


def _pair_cols(q, h):
    return q[:, (h // 2) * LANES:(h // 2 + 1) * LANES]


def _pair_tiles(per_head):
    lo = _lo_lanes()
    return jnp.concatenate([jnp.where(lo, per_head[2 * p], per_head[2 * p + 1])
                            for p in range(len(per_head) // 2)], axis=-1)


def _finish(oa, ob, za, zb, ga, gb, x, p, woa_ref, wob_ref, wout_ref, gple_ref, wpg_ref, wple_ref,
            fillers=()):
    fillers = list(fillers)
    za = za.astype(_F32)
    zb = zb.astype(_F32)
    br_a = jnp.dot((oa * jax.nn.silu(za)).astype(_BF16), woa_ref[...], preferred_element_type=_F32)
    br_b = jnp.dot((ob * jax.nn.silu(zb)).astype(_BF16), wob_ref[...], preferred_element_type=_F32)
    mrg = jax.nn.sigmoid(ga.astype(_F32)) * br_a + jax.nn.sigmoid(gb.astype(_F32)) * br_b
    h2 = x + jnp.dot(mrg.astype(_BF16), wout_ref[...], preferred_element_type=_F32)
    ple = jnp.dot(p.astype(_BF16), wple_ref[...], preferred_element_type=_F32)
    for f in fillers[:len(fillers) // 2]:
        f()
    gate = jax.nn.sigmoid(jnp.dot(_rmsnorm(h2, gple_ref[...]).astype(_BF16), wpg_ref[...],
                                  preferred_element_type=_F32))
    for f in fillers[len(fillers) // 2:]:
        f()
    return h2 + gate * ple


def _win_col(h):
    return LANES * (h // (WIN_HEADS // WIN_KV_HEADS))


def _band_col(h):
    return LANES * (h // 2)


def _prompt_kernel(xc_ref, xn_ref, p_ref, gin_ref, w_ref, gqa_ref, gka_ref, gqb_ref, gkb_ref,
                   tat_ref, tbt_ref, sink_ref,
                   woa_ref, wob_ref, wout_ref, gple_ref, wpg_ref, wple_ref,
                   o_ref, kat_ref, vat_ref, kbt_ref, vbt_ref,
                   qat_ref, qbt_ref, zg_ref, kbr_ref, vbr_ref, kbn_ref, vbn_ref,
                   kaw_ref, vaw_ref, kan_ref, van_ref):
    step = pl.program_id(1)
    rows = ATTN_TILE
    n_blk = rows // WIN_BLOCK
    ring = kbr_ref.shape[0]
    n_hist = BAND // rows

    def projection(x_tile, slot, kb_dst, vb_dst, ka_dst, va_dst):
        cache = {}

        def proj(a, b):
            if not cache:
                cache["xn"] = _rmsnorm(x_tile(), gin_ref[...]).astype(_BF16)
            return jnp.dot(cache["xn"], w_ref[:, a:b], preferred_element_type=_F32)

        def qa_cols(a):
            q = _head_rmsnorm(proj(a, a + FILL_CHUNK), gqa_ref[...]) * _Q_SCALE
            qat_ref[slot, a - _C_QA:a - _C_QA + FILL_CHUNK, :] = q.T.astype(_BF16)

        def qb_cols(a):
            q = _head_rmsnorm(proj(a, a + FILL_CHUNK), gqb_ref[...]) * _Q_SCALE
            qbt_ref[slot, a - _C_QB:a - _C_QB + FILL_CHUNK, :] = q.T.astype(_BF16)

        def kb_cols(a):
            k = _head_rmsnorm(proj(a, a + FILL_CHUNK), gkb_ref[...])
            kb_dst[:, a - _C_KB:a - _C_KB + FILL_CHUNK] = k.astype(_BF16)
            kbt_ref[0, :, a - _C_KB:a - _C_KB + FILL_CHUNK] = k

        def vb_cols(a):
            v = proj(a, a + FILL_CHUNK)
            vb_dst[a - _C_VB:a - _C_VB + FILL_CHUNK, :] = v.T.astype(_BF16)
            vbt_ref[0, :, a - _C_VB:a - _C_VB + FILL_CHUNK] = v

        def kva_cols():
            kva = proj(_C_KA, _C_ZA)
            ka_new = _head_rmsnorm(kva[:, :LANES], gka_ref[...])
            va_new = kva[:, LANES:]
            ka_dst[...] = _dup_heads(ka_new).astype(_BF16)
            va_dst[...] = va_new.T.astype(_BF16)
            kat_ref[0] = ka_new[rows - WIN:]
            vat_ref[0] = va_new[rows - WIN:]

        def gate_cols(a, dst):
            zg_ref[slot, :, dst:dst + FILL_CHUNK] = proj(a, a + FILL_CHUNK).astype(zg_ref.dtype)

        part = functools.partial
        return ([part(qb_cols, a) for a in range(_C_QB, _C_KB, FILL_CHUNK)]
                + [part(kb_cols, a) for a in range(_C_KB, _C_VB, FILL_CHUNK)]
                + [part(qa_cols, a) for a in range(_C_QA, _C_KA, FILL_CHUNK)]
                + [kva_cols]
                + [part(vb_cols, a) for a in range(_C_VB, _C_ZB, FILL_CHUNK)]
                + [part(gate_cols, a, a - _C_ZA) for a in range(_C_ZA, _C_QB, FILL_CHUNK)]
                + [part(gate_cols, a, a - _C_ZB + 512) for a in range(_C_ZB, _C_END, FILL_CHUNK)])

    @pl.when(step == 0)
    def _():
        for j in range(1, n_hist + 1):
            kbr_ref[ring - j] = jnp.zeros(kbr_ref.shape[1:], _BF16)
            vbr_ref[ring - j] = jnp.zeros(vbr_ref.shape[1:], _BF16)
        kaw_ref[0:WIN] = jnp.zeros((WIN, kaw_ref.shape[1]), _BF16)
        vaw_ref[:, 0:WIN] = jnp.zeros((vaw_ref.shape[0], WIN), _BF16)
        for f in projection(lambda: xc_ref[0, 0:rows], 0, kbr_ref.at[0], vbr_ref.at[0],
                            kaw_ref.at[WIN:], vaw_ref.at[:, WIN:]):
            f()

    heads_per_kv = WIN_HEADS // WIN_KV_HEADS
    lane = lax.broadcasted_iota(jnp.int32, (1, 2 * WIN_BLOCK), 1)

    def attend_block(sb):
        i = step * STEP_BLOCKS + sb
        cur = sb % 2
        lo, hi = sb * rows, (sb + 1) * rows
        if sb + 1 < STEP_BLOCKS:
            x_next = lambda: xc_ref[0, hi:hi + rows]
        else:
            x_next = lambda: xn_ref[0]
        fillers = projection(x_next, 1 - cur, kbn_ref, vbn_ref, kan_ref, van_ref)

        def band_bias(h, j):
            b = tbt_ref[h, j * rows:(j + 1) * rows, :]
            return b if j == n_hist else jnp.where(i >= n_hist - j, b, NEG_INF)

        def win_bias(pair, j):
            b = tat_ref[pair]
            if j > 0:
                return b
            return jnp.concatenate([jnp.where(i >= 1, b[:WIN], NEG_INF), b[WIN:]], axis=0)

        n_late = FILL_FINISH
        slots = [(i + ring - n_hist + j) % ring for j in range(n_hist + 1)]
        tasks = []
        for h in range(BAND_HEADS):
            c = _band_col(h)
            tasks.append((
                _masked_rows(qbt_ref[cur, c:c + LANES, :], h % 2),
                [functools.partial(lambda s, c: kbr_ref[s, :, c:c + LANES], s, c) for s in slots],
                [functools.partial(lambda s, h: vbr_ref[s, h * HEAD_DIM:(h + 1) * HEAD_DIM, :], s, h)
                 for s in slots],
                [functools.partial(band_bias, h, j) for j in range(n_hist + 1)],
                None, True))
            pair, j = h // n_blk, h % n_blk
            kv = 2 * pair // heads_per_kv
            r0 = j * WIN_BLOCK
            qtp = qat_ref[cur, pair * LANES:(pair + 1) * LANES, r0:r0 + WIN_BLOCK]
            tasks.append((
                jnp.concatenate([_masked_rows(qtp, 0), _masked_rows(qtp, 1)], axis=1),
                [functools.partial(
                    lambda r0, kv: kaw_ref[r0:r0 + WIN_BLOCK + WIN, kv * LANES:(kv + 1) * LANES], r0, kv)],
                [functools.partial(
                    lambda r0, kv: vaw_ref[kv * HEAD_DIM:(kv + 1) * HEAD_DIM, r0:r0 + WIN_BLOCK + WIN], r0, kv)],
                [functools.partial(win_bias, pair, j)],
                jnp.where(lane < WIN_BLOCK, sink_ref[2 * pair], sink_ref[2 * pair + 1]) * LOG2E, True))
        outs = _attention(tasks, QK_LOOKAHEAD, fillers[:len(fillers) - n_late], FILL_FRONT)
        ob = jnp.concatenate(outs[0::2], axis=0).T
        wouts = outs[1::2]
        oa = jnp.concatenate(
            [jnp.concatenate([wouts[(h // 2) * n_blk + j][:, (h % 2) * WIN_BLOCK:(h % 2 + 1) * WIN_BLOCK]
                              for j in range(n_blk)], axis=1) for h in range(WIN_HEADS)], axis=0).T
        o_ref[0, lo:hi] = _finish(
            oa, ob, zg_ref[cur, :, 0:512], zg_ref[cur, :, 512:1024], zg_ref[cur, :, 1024:2048],
            zg_ref[cur, :, 2048:3072], xc_ref[0, lo:hi], p_ref[0, lo:hi],
            woa_ref, wob_ref, wout_ref, gple_ref, wpg_ref, wple_ref, fillers[len(fillers) - n_late:])

        kbr_ref[(i + 1) % ring] = kbn_ref[...]
        vbr_ref[(i + 1) % ring] = vbn_ref[...]
        kaw_ref[0:WIN] = kaw_ref[rows:rows + WIN]
        vaw_ref[:, 0:WIN] = vaw_ref[:, rows:rows + WIN]
        kaw_ref[WIN:] = kan_ref[...]
        vaw_ref[:, WIN:] = van_ref[...]

    for sb in range(STEP_BLOCKS):
        attend_block(sb)


def _sample_attn_kernel(qa_ref, qb_ref, kac_ref, kan_ref, vac_ref, van_ref,
                        kbc_ref, kbn_ref, vbc_ref, vbn_ref,
                        za_ref, zb_ref, ga_ref, gb_ref, x_ref, p_ref, ta_ref, tb_ref, sink_ref,
                        woa_ref, wob_ref, wout_ref, gple_ref, wpg_ref, wple_ref, o_ref):
    nseq, rows = qa_ref.shape[:2]
    half = kbc_ref.shape[1] // 2
    a_edges = (0, kac_ref.shape[1], kac_ref.shape[1] + rows)
    b_edges = (0, half, 2 * half, 2 * half + rows)

    def pieces(s, cache_ref, new_ref, c, splits):
        n = cache_ref.shape[1] // splits
        return ([functools.partial(lambda j: cache_ref[s, j * n:(j + 1) * n, c:c + LANES], j)
                 for j in range(splits)] + [lambda: new_ref[s, :, c:c + LANES]])

    tasks = []
    for h in range(BAND_HEADS):
        for s in range(nseq):
            c = _band_col(h)
            tasks.append((_masked_q(_pair_cols(qb_ref[s], h), h % 2),
                          pieces(s, kbc_ref, kbn_ref, c, 2), pieces(s, vbc_ref, vbn_ref, c, 2),
                          [functools.partial(lambda h, j: tb_ref[h, :, b_edges[j]:b_edges[j + 1]], h, j)
                           for j in range(3)], None, False))
            c = _win_col(h)
            tasks.append((_masked_q(_pair_cols(qa_ref[s], h), h % 2),
                          pieces(s, kac_ref, kan_ref, c, 1), pieces(s, vac_ref, van_ref, c, 1),
                          [functools.partial(lambda h, j: ta_ref[h, :, a_edges[j]:a_edges[j + 1]], h, j)
                           for j in range(2)], sink_ref[h] * LOG2E, False))
    outs = _attention(tasks, QK_LOOKAHEAD)
    per_head = 2 * nseq
    seq_rows = lambda t: jnp.concatenate(
        [_pair_tiles([outs[h * per_head + 2 * s + t] for h in range(BAND_HEADS)]) for s in range(nseq)], axis=0)
    merge = lambda ref: ref[...].reshape(nseq * rows, ref.shape[-1])
    out = _finish(seq_rows(1), seq_rows(0), merge(za_ref), merge(zb_ref), merge(ga_ref), merge(gb_ref),
                  merge(x_ref), merge(p_ref), woa_ref, wob_ref, wout_ref, gple_ref, wpg_ref, wple_ref)
    o_ref[...] = out.reshape(o_ref.shape)


def _weight_specs(d, ple):
    return [_const_spec((512, d)), _const_spec((512, d)), _const_spec((d, d)),
            _const_spec((1, d)), _const_spec((d, d)), _const_spec((ple, d))]


def _prompt_layer(x, p, proj_args, tat, tbt, sink, weights):
    gin, w_in, gqa, gka, gqb, gkb = proj_args
    bsz, seq, d = x.shape
    tq = ATTN_TILE
    n_hist = BAND // tq
    n_blocks = seq // tq
    cur = lambda w: pl.BlockSpec((1, STEP_BLOCKS * tq, w), lambda b, i: (b, i, 0))
    nxt = lambda w: pl.BlockSpec((1, tq, w), lambda b, i: (
        b, jnp.minimum(STEP_BLOCKS * (i + 1), n_blocks - 1), 0))
    tail = lambda r, w: pl.BlockSpec((1, r, w), lambda b, i: (b, 0, 0))
    assert (n_blocks - n_hist) % STEP_BLOCKS == 0
    btail = pl.BlockSpec((1, tq, 512), lambda b, i: (
        b, jnp.clip(STEP_BLOCKS * i + 1 - (n_blocks - n_hist), 0, n_hist - 1), 0))
    return pl.pallas_call(
        _prompt_kernel,
        grid=(bsz, n_blocks // STEP_BLOCKS),
        in_specs=[cur(d), nxt(d), cur(p.shape[-1]), _const_spec((1, d)), _const_spec(w_in.shape),
                  _const_spec((1, LANES)), _const_spec((1, LANES)),
                  _const_spec((1, LANES)), _const_spec((1, LANES)),
                  _const_spec(tat.shape), _const_spec(tbt.shape),
                  pl.BlockSpec(memory_space=pltpu.SMEM)]
                 + _weight_specs(d, p.shape[-1]),
        out_specs=[cur(d), tail(WIN, 128), tail(WIN, 128), btail, btail],
        out_shape=[jax.ShapeDtypeStruct((bsz, seq, d), _F32),
                   jax.ShapeDtypeStruct((bsz, WIN, 128), _F32),
                   jax.ShapeDtypeStruct((bsz, WIN, 128), _F32),
                   jax.ShapeDtypeStruct((bsz, BAND, 512), _F32),
                   jax.ShapeDtypeStruct((bsz, BAND, 512), _F32)],
        scratch_shapes=[pltpu.VMEM((2, 512, tq), _BF16), pltpu.VMEM((2, 512, tq), _BF16),
                        pltpu.VMEM((2, tq, _C_END - _C_ZB + _C_QB - _C_ZA), _BF16),
                        pltpu.VMEM((n_hist + 2, tq, 512), _BF16), pltpu.VMEM((n_hist + 2, 512, tq), _BF16),
                        pltpu.VMEM((tq, 512), _BF16), pltpu.VMEM((512, tq), _BF16),
                        pltpu.VMEM((WIN + tq, 256), _BF16), pltpu.VMEM((LANES, WIN + tq), _BF16),
                        pltpu.VMEM((tq, 256), _BF16), pltpu.VMEM((LANES, tq), _BF16)],
        compiler_params=pltpu.CompilerParams(
            dimension_semantics=("arbitrary", "arbitrary"),
            vmem_limit_bytes=VMEM_LIMIT_BYTES),
        name="prompt_layer",
    )(x, x, p, gin, w_in, gqa, gka, gqb, gkb, tat, tbt, sink, *weights)


def _sample_attend(acts, caches, x, p, ta, tb, sink, weights):
    qa, ka, va, za, qb, kb, vb, zb, ga, gb = acts
    cka, cva, ckb, cvb = caches
    bsz, seq, d = x.shape
    nseq = math.gcd(bsz, SAMPLE_SEQS)
    blk = lambda a: pl.BlockSpec((nseq,) + a.shape[1:], lambda b: (b, 0, 0))
    ins = (qa, qb, cka, ka, cva, va, ckb, kb, cvb, vb, za, zb, ga, gb, x, p)
    ta_spec = _const_spec((ta.shape[0], seq, ta.shape[2]))
    tb_spec = _const_spec((tb.shape[0], seq, tb.shape[2]))
    return pl.pallas_call(
        _sample_attn_kernel,
        grid=(bsz // nseq,),
        in_specs=[blk(a) for a in ins] + [ta_spec, tb_spec, pl.BlockSpec(memory_space=pltpu.SMEM)]
                 + _weight_specs(d, p.shape[-1]),
        out_specs=blk(x),
        out_shape=jax.ShapeDtypeStruct((bsz, seq, d), _F32),
        compiler_params=pltpu.CompilerParams(
            dimension_semantics=("arbitrary",),
            vmem_limit_bytes=VMEM_LIMIT_BYTES),
        name="sample_attn",
    )(*ins, ta, tb, sink, *weights)


def _dup_cache(c):
    return jnp.concatenate([c[:, :, 0], c[:, :, 0], c[:, :, 1], c[:, :, 1]], axis=-1).astype(_BF16)


def kernel(x_prompt, x_sample, cache_k_win, cache_v_win, cache_k_band, cache_v_band, p_prompt, p_sample, g_in, w_in, g_q_win, g_k_win, sink_win, g_q_band, g_k_band, rel_bias_band, w_o_win, w_o_band, w_out, g_ple, w_ple_gate, w_ple):
    depth = w_in.shape[0]
    bsz, seq, d = x_prompt.shape
    dbsz, dseq, _ = x_sample.shape
    assert seq % (STEP_BLOCKS * ATTN_TILE) == 0 and seq >= BAND + ATTN_TILE and dseq == CHUNK
    assert STEP_BLOCKS % 2 == 0
    assert ATTN_TILE % WIN_BLOCK == 0 and WIN_BLOCK == WIN and BAND % ATTN_TILE == 0
    assert BAND_HEADS == (WIN_HEADS // 2) * (ATTN_TILE // WIN_BLOCK)
    assert cache_k_win.shape[2] == WIN and cache_k_band.shape[2] == BAND

    h_p, h_s = x_prompt, x_sample
    outs = [[] for _ in range(8)]
    ta, tat = _win_bias_table(WIN_HEADS)
    for i in range(depth):
        pair = lambda g: jnp.tile(g[i], 2)[None, :]
        proj_args = (g_in[i][None, :], w_in[i].astype(_BF16),
                     pair(g_q_win), pair(g_k_win), pair(g_q_band), pair(g_k_band))
        weights = (w_o_win[i].astype(_BF16), w_o_band[i].astype(_BF16), w_out[i].astype(_BF16),
                   g_ple[i][None, :], w_ple_gate[i].astype(_BF16), w_ple[i].astype(_BF16))
        tb, tbt = _band_bias_table(rel_bias_band[i], ATTN_TILE)

        h_p, kat, vat, kbt, vbt = _prompt_layer(h_p, p_prompt[i], proj_args, tat, tbt, sink_win[i], weights)
        outs[0].append(kat.reshape(bsz, WIN, WIN_KV_HEADS, HEAD_DIM))
        outs[1].append(vat.reshape(bsz, WIN, WIN_KV_HEADS, HEAD_DIM))
        outs[2].append(kbt.reshape(bsz, BAND, BAND_HEADS, HEAD_DIM))
        outs[3].append(vbt.reshape(bsz, BAND, BAND_HEADS, HEAD_DIM))

        ntok = dbsz * dseq
        *acts, kan, van, kbn, vbn = _project(h_s.reshape(1, ntok, d), *proj_args,
                                             tile=ntok, tail_a=ntok, tail_b=ntok)
        acts = [a.reshape(dbsz, dseq, a.shape[-1]) for a in acts]
        caches = (_dup_cache(cache_k_win[i]), _dup_cache(cache_v_win[i]),
                  cache_k_band[i].reshape(dbsz, BAND, -1).astype(_BF16),
                  cache_v_band[i].reshape(dbsz, BAND, -1).astype(_BF16))
        h_s = _sample_attend(acts, caches, h_s, p_sample[i], ta, tb, sink_win[i], weights)
        new = lambda t, nh: t.reshape(dbsz, dseq, nh, HEAD_DIM)
        outs[4].append(jnp.concatenate([cache_k_win[i][:, dseq:], new(kan, WIN_KV_HEADS)], axis=1))
        outs[5].append(jnp.concatenate([cache_v_win[i][:, dseq:], new(van, WIN_KV_HEADS)], axis=1))
        outs[6].append(jnp.concatenate([cache_k_band[i][:, dseq:], new(kbn, BAND_HEADS)], axis=1))
        outs[7].append(jnp.concatenate([cache_v_band[i][:, dseq:], new(vbn, BAND_HEADS)], axis=1))
    return (h_p, h_s) + tuple(jnp.stack(o) for o in outs)
```

```python
import functools
import math

import jax
import jax.numpy as jnp
from jax import lax
from jax.experimental import pallas as pl
from jax.experimental.pallas import tpu as pltpu

CHUNK = 64
HEAD_DIM = 64
WIN_HEADS = 8
WIN_KV_HEADS = 2
WIN_CHUNKS = 2
BAND_HEADS = 8
BAND_CHUNKS = 8
REL_CLIP = 128
ALIBI_MAX = 8.0
EPS = 1e-6
NEG_INF = -1e30
LOG2E = math.log2(math.e)

WIN = WIN_CHUNKS * CHUNK
BAND = BAND_CHUNKS * CHUNK
LANES = 128
ATTN_TILE = 256
STEP_BLOCKS = 2
SAMPLE_SEQS = 4
WIN_BLOCK = 128
QK_LOOKAHEAD = 3
FILL_CHUNK = 256
FILL_FRONT = 4
FILL_FINISH = 3
VMEM_LIMIT_BYTES = 56 * 1024 * 1024

_F32 = jnp.float32
_BF16 = jnp.bfloat16
_NT = (((1,), (1,)), ((), ()))


def _rmsnorm(x, g):
    ms = jnp.mean(x * x, axis=-1, keepdims=True)
    return x * lax.rsqrt(ms + EPS) * g


def _lo_lanes(rows=1):
    return lax.broadcasted_iota(jnp.int32, (rows, LANES), 1) < HEAD_DIM


def _head_rmsnorm(u, g_pair):
    lo = _lo_lanes()
    outs = []
    for j in range(u.shape[1] // LANES):
        t = u[:, j * LANES:(j + 1) * LANES]
        sq = t * t
        s_lo = jnp.sum(jnp.where(lo, sq, 0.0), axis=-1, keepdims=True)
        s_hi = jnp.sum(jnp.where(lo, 0.0, sq), axis=-1, keepdims=True)
        ms = jnp.where(lo, s_lo, s_hi) * (1.0 / HEAD_DIM)
        outs.append(t * lax.rsqrt(ms + EPS) * g_pair)
    return outs[0] if len(outs) == 1 else jnp.concatenate(outs, axis=-1)


def _dup_heads(t):
    lo = _lo_lanes()
    r = pltpu.roll(t, HEAD_DIM, axis=1)
    return jnp.concatenate([jnp.where(lo, t, r), jnp.where(lo, r, t)], axis=-1)


_C_QA, _C_KA, _C_VA, _C_ZA, _C_QB, _C_KB, _C_VB, _C_ZB, _C_GA, _C_GB, _C_END = (
    0, 512, 640, 768, 1280, 1792, 2304, 2816, 3328, 4352, 5376)
_Q_SCALE = HEAD_DIM ** -0.5 * LOG2E


def _proj_kernel(x_ref, gin_ref, w_ref, gqa_ref, gka_ref, gqb_ref, gkb_ref,
                 qa_ref, ka_ref, va_ref, za_ref, qb_ref, kb_ref, vb_ref, zb_ref, ga_ref, gb_ref,
                 kat_ref, vat_ref, kbt_ref, vbt_ref):
    last = pl.program_id(1) == pl.num_programs(1) - 1
    xn = _rmsnorm(x_ref[0], gin_ref[...]).astype(_BF16)

    def proj(a, b):
        return jnp.dot(xn, w_ref[:, a:b], preferred_element_type=_F32)

    qa_ref[0] = (_head_rmsnorm(proj(_C_QA, _C_KA), gqa_ref[...]) * _Q_SCALE).astype(_BF16)
    qb_ref[0] = (_head_rmsnorm(proj(_C_QB, _C_KB), gqb_ref[...]) * _Q_SCALE).astype(_BF16)

    ka = _head_rmsnorm(proj(_C_KA, _C_VA), gka_ref[...])
    va = proj(_C_VA, _C_ZA)
    ka_ref[0] = _dup_heads(ka).astype(_BF16)
    va_ref[0] = _dup_heads(va).astype(_BF16)
    kb = _head_rmsnorm(proj(_C_KB, _C_VB), gkb_ref[...])
    vb = proj(_C_VB, _C_ZB)
    kb_ref[0] = kb.astype(_BF16)
    vb_ref[0] = vb.astype(_BF16)

    za_ref[0] = proj(_C_ZA, _C_QB).astype(_BF16)
    zb_ref[0] = proj(_C_ZB, _C_GA).astype(_BF16)
    ga_ref[0] = proj(_C_GA, _C_GB).astype(_BF16)
    gb_ref[0] = proj(_C_GB, _C_END).astype(_BF16)

    @pl.when(last)
    def _():
        ta, tb = kat_ref.shape[1], kbt_ref.shape[1]
        t = ka.shape[0]
        kat_ref[0] = ka[t - ta:, :]
        vat_ref[0] = va[t - ta:, :]
        kbt_ref[0] = kb[t - tb:, :]
        vbt_ref[0] = vb[t - tb:, :]


def _const_spec(shape, index=None):
    index = (0,) * len(shape) if index is None else index
    return pl.BlockSpec(shape, lambda *_: index, pipeline_mode=pl.Buffered(1))


def _project(x, gin, w_in, gqa, gka, gqb, gkb, *, tile, tail_a, tail_b):
    bsz, seq, d = x.shape
    nt = seq // tile
    row = lambda w: pl.BlockSpec((1, tile, w), lambda b, t: (b, t, 0))
    tail = lambda r, w: pl.BlockSpec((1, r, w), lambda b, t: (b, 0, 0))
    act = lambda w: jax.ShapeDtypeStruct((bsz, seq, w), _BF16)
    widths = (512, 256, 256, 512, 512, 512, 512, 512, 1024, 1024)
    return pl.pallas_call(
        _proj_kernel,
        grid=(bsz, nt),
        in_specs=[row(d), _const_spec((1, d)), _const_spec(w_in.shape),
                  _const_spec((1, LANES)), _const_spec((1, LANES)),
                  _const_spec((1, LANES)), _const_spec((1, LANES))],
        out_specs=[row(w) for w in widths] + [tail(tail_a, 128), tail(tail_a, 128),
                                              tail(tail_b, 512), tail(tail_b, 512)],
        out_shape=[act(w) for w in widths] + [
            jax.ShapeDtypeStruct((bsz, tail_a, 128), _F32),
            jax.ShapeDtypeStruct((bsz, tail_a, 128), _F32),
            jax.ShapeDtypeStruct((bsz, tail_b, 512), _F32),
            jax.ShapeDtypeStruct((bsz, tail_b, 512), _F32)],
        compiler_params=pltpu.CompilerParams(
            dimension_semantics=("arbitrary", "arbitrary"),
            vmem_limit_bytes=VMEM_LIMIT_BYTES),
        name="proj",
    )(x, gin, w_in, gqa, gka, gqb, gkb)


def _win_bias_kernel(o_ref, ot_ref):
    heads, rows, width = o_ref.shape
    r = lax.broadcasted_iota(jnp.int32, (rows, width), 0)
    c = lax.broadcasted_iota(jnp.int32, (rows, width), 1)
    dist = jnp.abs(r + WIN - c).astype(_F32)
    qc, kc = r // CHUNK, c // CHUNK
    valid = jnp.logical_and(kc >= qc, kc <= qc + WIN_CHUNKS)
    for h in range(heads):
        slope = 2.0 ** (-ALIBI_MAX * (h + 1) / heads)
        bias = jnp.where(valid, dist * (-slope * LOG2E), NEG_INF)
        o_ref[h] = bias
        ot_ref[h // 2, :, (h % 2) * rows:(h % 2 + 1) * rows] = bias.T


def _win_bias_table(heads):
    return pl.pallas_call(
        _win_bias_kernel,
        out_shape=[jax.ShapeDtypeStruct((heads, WIN_BLOCK, WIN_BLOCK + WIN), _F32),
                   jax.ShapeDtypeStruct((heads // 2, WIN_BLOCK + WIN, 2 * WIN_BLOCK), _F32)],
        name="win_bias",
    )()


def _band_bias_kernel(w_ref, o_ref, ot_ref):
    width, rows = ot_ref.shape[1:]
    t = pltpu.roll(jnp.broadcast_to(w_ref[0], (rows, width)), 0, 1, stride=1, stride_axis=0)
    qc = lax.broadcasted_iota(jnp.int32, (rows, width), 0) // CHUNK
    kc = lax.broadcasted_iota(jnp.int32, (rows, width), 1) // CHUNK
    valid = jnp.logical_and(kc >= qc, kc <= qc + BAND_CHUNKS)
    bias = jnp.where(valid, t * LOG2E, NEG_INF)
    o_ref[0] = bias[:o_ref.shape[1]]
    ot_ref[0] = bias.T


def _band_bias_table(table, rows):
    heads = table.shape[0]
    width = rows + BAND
    assert rows >= 2 * CHUNK and rows % CHUNK == 0
    far = table[:, 2 * REL_CLIP:]
    w = jnp.concatenate([jnp.broadcast_to(far, (heads, BAND - REL_CLIP)), table[:, ::-1],
                         jnp.broadcast_to(far, (heads, width - BAND - REL_CLIP - 1))], axis=1)
    return pl.pallas_call(
        _band_bias_kernel,
        grid=(heads,),
        in_specs=[pl.BlockSpec((1, 1, width), lambda h: (h, 0, 0))],
        out_specs=[pl.BlockSpec((1, CHUNK, width), lambda h: (h, 0, 0)),
                   pl.BlockSpec((1, width, rows), lambda h: (h, 0, 0))],
        out_shape=[jax.ShapeDtypeStruct((heads, CHUNK, width), _F32),
                   jax.ShapeDtypeStruct((heads, width, rows), _F32)],
        name="band_bias",
    )(w.astype(_F32)[:, None, :])


def _reduce_keys(pieces, elementwise, reduce, axis):
    by_size = {}
    for x in pieces:
        n = x.shape[axis]
        by_size[n] = x if n not in by_size else elementwise(by_size[n], x)
    return functools.reduce(elementwise, [reduce(x, axis=axis, keepdims=True)
                                          for x in by_size.values()])


def _attention(tasks, lookahead, fillers=(), front=0):
    def scores(task):
        q, ks, _, bs, _, keys_on_rows = task
        if keys_on_rows:
            return [jnp.dot(k(), q, preferred_element_type=_F32) + b() for k, b in zip(ks, bs)]
        return [lax.dot_general(q, k(), _NT, preferred_element_type=_F32) + b()
                for k, b in zip(ks, bs)]

    pending = [scores(t) for t in tasks[:lookahead]]
    outs = []
    filled = 0
    for idx, (_, _, vs, _, sink, keys_on_rows) in enumerate(tasks):
        if idx + lookahead < len(tasks):
            pending.append(scores(tasks[idx + lookahead]))
        due = front + -(-(idx + 1) * (len(fillers) - front) // len(tasks))
        while filled < min(due, len(fillers)):
            fillers[filled]()
            filled += 1
        s = pending[idx]
        pending[idx] = None
        axis = 0 if keys_on_rows else 1
        m = _reduce_keys(s, jnp.maximum, jnp.max, axis)
        if sink is not None:
            m = jnp.maximum(m, sink)
        e = [jnp.exp2(sj - m) for sj in s]
        den = _reduce_keys(e, jnp.add, jnp.sum, axis)
        if sink is not None:
            den = den + jnp.exp2(sink - m)
        if keys_on_rows:
            pv = [jnp.dot(v(), ej.astype(_BF16), preferred_element_type=_F32) for ej, v in zip(e, vs)]
        else:
            pv = [jnp.dot(ej.astype(_BF16), v(), preferred_element_type=_F32) for ej, v in zip(e, vs)]
        outs.append(functools.reduce(jnp.add, pv) / den)
    return outs


def _masked_q(qp, half):
    keep = _lo_lanes() if half == 0 else jnp.logical_not(_lo_lanes())
    return jnp.where(keep, qp, jnp.zeros_like(qp))


def _masked_rows(qtp, half):
    lo = lax.broadcasted_iota(jnp.int32, qtp.shape, 0) < HEAD_DIM
    return jnp.where(lo if half == 0 else jnp.logical_not(lo), qtp, jnp.zeros_like(qtp))


def _pair_cols(q, h):
    return q[:, (h // 2) * LANES:(h // 2 + 1) * LANES]


def _pair_tiles(per_head):
    lo = _lo_lanes()
    return jnp.concatenate([jnp.where(lo, per_head[2 * p], per_head[2 * p + 1])
                            for p in range(len(per_head) // 2)], axis=-1)


def _finish(oa, ob, za, zb, ga, gb, x, p, woa_ref, wob_ref, wout_ref, gple_ref, wpg_ref, wple_ref,
            fillers=()):
    fillers = list(fillers)
    za = za.astype(_F32)
    zb = zb.astype(_F32)
    br_a = jnp.dot((oa * jax.nn.silu(za)).astype(_BF16), woa_ref[...], preferred_element_type=_F32)
    br_b = jnp.dot((ob * jax.nn.silu(zb)).astype(_BF16), wob_ref[...], preferred_element_type=_F32)
    mrg = jax.nn.sigmoid(ga.astype(_F32)) * br_a + jax.nn.sigmoid(gb.astype(_F32)) * br_b
    h2 = x + jnp.dot(mrg.astype(_BF16), wout_ref[...], preferred_element_type=_F32)
    ple = jnp.dot(p.astype(_BF16), wple_ref[...], preferred_element_type=_F32)
    for f in fillers[:len(fillers) // 2]:
        f()
    gate = jax.nn.sigmoid(jnp.dot(_rmsnorm(h2, gple_ref[...]).astype(_BF16), wpg_ref[...],
                                  preferred_element_type=_F32))
    for f in fillers[len(fillers) // 2:]:
        f()
    return h2 + gate * ple


def _win_col(h):
    return LANES * (h // (WIN_HEADS // WIN_KV_HEADS))


def _band_col(h):
    return LANES * (h // 2)


def _prompt_kernel(xc_ref, xn_ref, p_ref, gin_ref, w_ref, gqa_ref, gka_ref, gqb_ref, gkb_ref,
                   tat_ref, tbt_ref, sink_ref,
                   woa_ref, wob_ref, wout_ref, gple_ref, wpg_ref, wple_ref,
                   o_ref, kat_ref, vat_ref, kbt_ref, vbt_ref,
                   qat_ref, qbt_ref, zg_ref, kbr_ref, vbr_ref, kbn_ref, vbn_ref,
                   kaw_ref, vaw_ref, kan_ref, van_ref):
    step = pl.program_id(1)
    rows = ATTN_TILE
    n_blk = rows // WIN_BLOCK
    ring = kbr_ref.shape[0]
    n_hist = BAND // rows

    def projection(x_tile, slot, kb_dst, vb_dst, ka_dst, va_dst):
        cache = {}

        def proj(a, b):
            if not cache:
                cache["xn"] = _rmsnorm(x_tile(), gin_ref[...]).astype(_BF16)
            return jnp.dot(cache["xn"], w_ref[:, a:b], preferred_element_type=_F32)

        def qa_cols(a):
            q = _head_rmsnorm(proj(a, a + FILL_CHUNK), gqa_ref[...]) * _Q_SCALE
            qat_ref[slot, a - _C_QA:a - _C_QA + FILL_CHUNK, :] = q.T.astype(_BF16)

        def qb_cols(a):
            q = _head_rmsnorm(proj(a, a + FILL_CHUNK), gqb_ref[...]) * _Q_SCALE
            qbt_ref[slot, a - _C_QB:a - _C_QB + FILL_CHUNK, :] = q.T.astype(_BF16)

        def kb_cols(a):
            k = _head_rmsnorm(proj(a, a + FILL_CHUNK), gkb_ref[...])
            kb_dst[:, a - _C_KB:a - _C_KB + FILL_CHUNK] = k.astype(_BF16)
            kbt_ref[0, :, a - _C_KB:a - _C_KB + FILL_CHUNK] = k

        def vb_cols(a):
            v = proj(a, a + FILL_CHUNK)
            vb_dst[a - _C_VB:a - _C_VB + FILL_CHUNK, :] = v.T.astype(_BF16)
            vbt_ref[0, :, a - _C_VB:a - _C_VB + FILL_CHUNK] = v

        def kva_cols():
            kva = proj(_C_KA, _C_ZA)
            ka_new = _head_rmsnorm(kva[:, :LANES], gka_ref[...])
            va_new = kva[:, LANES:]
            ka_dst[...] = _dup_heads(ka_new).astype(_BF16)
            va_dst[...] = va_new.T.astype(_BF16)
            kat_ref[0] = ka_new[rows - WIN:]
            vat_ref[0] = va_new[rows - WIN:]

        def gate_cols(a, dst):
            zg_ref[slot, :, dst:dst + FILL_CHUNK] = proj(a, a + FILL_CHUNK).astype(zg_ref.dtype)

        part = functools.partial
        return ([part(qb_cols, a) for a in range(_C_QB, _C_KB, FILL_CHUNK)]
                + [part(kb_cols, a) for a in range(_C_KB, _C_VB, FILL_CHUNK)]
                + [part(qa_cols, a) for a in range(_C_QA, _C_KA, FILL_CHUNK)]
                + [kva_cols]
                + [part(vb_cols, a) for a in range(_C_VB, _C_ZB, FILL_CHUNK)]
                + [part(gate_cols, a, a - _C_ZA) for a in range(_C_ZA, _C_QB, FILL_CHUNK)]
                + [part(gate_cols, a, a - _C_ZB + 512) for a in range(_C_ZB, _C_END, FILL_CHUNK)])

    @pl.when(step == 0)
    def _():
        for j in range(1, n_hist + 1):
            kbr_ref[ring - j] = jnp.zeros(kbr_ref.shape[1:], _BF16)
            vbr_ref[ring - j] = jnp.zeros(vbr_ref.shape[1:], _BF16)
        kaw_ref[0:WIN] = jnp.zeros((WIN, kaw_ref.shape[1]), _BF16)
        vaw_ref[:, 0:WIN] = jnp.zeros((vaw_ref.shape[0], WIN), _BF16)
        for f in projection(lambda: xc_ref[0, 0:rows], 0, kbr_ref.at[0], vbr_ref.at[0],
                            kaw_ref.at[WIN:], vaw_ref.at[:, WIN:]):
            f()

    heads_per_kv = WIN_HEADS // WIN_KV_HEADS
    lane = lax.broadcasted_iota(jnp.int32, (1, 2 * WIN_BLOCK), 1)

    def attend_block(sb):
        i = step * STEP_BLOCKS + sb
        cur = sb % 2
        lo, hi = sb * rows, (sb + 1) * rows
        if sb + 1 < STEP_BLOCKS:
            x_next = lambda: xc_ref[0, hi:hi + rows]
        else:
            x_next = lambda: xn_ref[0]
        fillers = projection(x_next, 1 - cur, kbn_ref, vbn_ref, kan_ref, van_ref)

        def band_bias(h, j):
            b = tbt_ref[h, j * rows:(j + 1) * rows, :]
            return b if j == n_hist else jnp.where(i >= n_hist - j, b, NEG_INF)

        def win_bias(pair, j):
            b = tat_ref[pair]
            if j > 0:
                return b
            return jnp.concatenate([jnp.where(i >= 1, b[:WIN], NEG_INF), b[WIN:]], axis=0)

        n_late = FILL_FINISH
        slots = [(i + ring - n_hist + j) % ring for j in range(n_hist + 1)]
        tasks = []
        for h in range(BAND_HEADS):
            c = _band_col(h)
            tasks.append((
                _masked_rows(qbt_ref[cur, c:c + LANES, :], h % 2),
                [functools.partial(lambda s, c: kbr_ref[s, :, c:c + LANES], s, c) for s in slots],
                [functools.partial(lambda s, h: vbr_ref[s, h * HEAD_DIM:(h + 1) * HEAD_DIM, :], s, h)
                 for s in slots],
                [functools.partial(band_bias, h, j) for j in range(n_hist + 1)],
                None, True))
            pair, j = h // n_blk, h % n_blk
            kv = 2 * pair // heads_per_kv
            r0 = j * WIN_BLOCK
            qtp = qat_ref[cur, pair * LANES:(pair + 1) * LANES, r0:r0 + WIN_BLOCK]
            tasks.append((
                jnp.concatenate([_masked_rows(qtp, 0), _masked_rows(qtp, 1)], axis=1),
                [functools.partial(
                    lambda r0, kv: kaw_ref[r0:r0 + WIN_BLOCK + WIN, kv * LANES:(kv + 1) * LANES], r0, kv)],
                [functools.partial(
                    lambda r0, kv: vaw_ref[kv * HEAD_DIM:(kv + 1) * HEAD_DIM, r0:r0 + WIN_BLOCK + WIN], r0, kv)],
                [functools.partial(win_bias, pair, j)],
                jnp.where(lane < WIN_BLOCK, sink_ref[2 * pair], sink_ref[2 * pair + 1]) * LOG2E, True))
        outs = _attention(tasks, QK_LOOKAHEAD, fillers[:len(fillers) - n_late], FILL_FRONT)
        ob = jnp.concatenate(outs[0::2], axis=0).T
        wouts = outs[1::2]
        oa = jnp.concatenate(
            [jnp.concatenate([wouts[(h // 2) * n_blk + j][:, (h % 2) * WIN_BLOCK:(h % 2 + 1) * WIN_BLOCK]
                              for j in range(n_blk)], axis=1) for h in range(WIN_HEADS)], axis=0).T
        o_ref[0, lo:hi] = _finish(
            oa, ob, zg_ref[cur, :, 0:512], zg_ref[cur, :, 512:1024], zg_ref[cur, :, 1024:2048],
            zg_ref[cur, :, 2048:3072], xc_ref[0, lo:hi], p_ref[0, lo:hi],
            woa_ref, wob_ref, wout_ref, gple_ref, wpg_ref, wple_ref, fillers[len(fillers) - n_late:])

        kbr_ref[(i + 1) % ring] = kbn_ref[...]
        vbr_ref[(i + 1) % ring] = vbn_ref[...]
        kaw_ref[0:WIN] = kaw_ref[rows:rows + WIN]
        vaw_ref[:, 0:WIN] = vaw_ref[:, rows:rows + WIN]
        kaw_ref[WIN:] = kan_ref[...]
        vaw_ref[:, WIN:] = van_ref[...]

    for sb in range(STEP_BLOCKS):
        attend_block(sb)


def _sample_attn_kernel(qa_ref, qb_ref, kac_ref, kan_ref, vac_ref, van_ref,
                        kbc_ref, kbn_ref, vbc_ref, vbn_ref,
                        za_ref, zb_ref, ga_ref, gb_ref, x_ref, p_ref, ta_ref, tb_ref, sink_ref,
                        woa_ref, wob_ref, wout_ref, gple_ref, wpg_ref, wple_ref, o_ref):
    nseq, rows = qa_ref.shape[:2]
    half = kbc_ref.shape[1] // 2
    a_edges = (0, kac_ref.shape[1], kac_ref.shape[1] + rows)
    b_edges = (0, half, 2 * half, 2 * half + rows)

    def pieces(s, cache_ref, new_ref, c, splits):
        n = cache_ref.shape[1] // splits
        return ([functools.partial(lambda j: cache_ref[s, j * n:(j + 1) * n, c:c + LANES], j)
                 for j in range(splits)] + [lambda: new_ref[s, :, c:c + LANES]])

    tasks = []
    for h in range(BAND_HEADS):
        for s in range(nseq):
            c = _band_col(h)
            tasks.append((_masked_q(_pair_cols(qb_ref[s], h), h % 2),
                          pieces(s, kbc_ref, kbn_ref, c, 2), pieces(s, vbc_ref, vbn_ref, c, 2),
                          [functools.partial(lambda h, j: tb_ref[h, :, b_edges[j]:b_edges[j + 1]], h, j)
                           for j in range(3)], None, False))
            c = _win_col(h)
            tasks.append((_masked_q(_pair_cols(qa_ref[s], h), h % 2),
                          pieces(s, kac_ref, kan_ref, c, 1), pieces(s, vac_ref, van_ref, c, 1),
                          [functools.partial(lambda h, j: ta_ref[h, :, a_edges[j]:a_edges[j + 1]], h, j)
                           for j in range(2)], sink_ref[h] * LOG2E, False))
    outs = _attention(tasks, QK_LOOKAHEAD)
    per_head = 2 * nseq
    seq_rows = lambda t: jnp.concatenate(
        [_pair_tiles([outs[h * per_head + 2 * s + t] for h in range(BAND_HEADS)]) for s in range(nseq)], axis=0)
    merge = lambda ref: ref[...].reshape(nseq * rows, ref.shape[-1])
    out = _finish(seq_rows(1), seq_rows(0), merge(za_ref), merge(zb_ref), merge(ga_ref), merge(gb_ref),
                  merge(x_ref), merge(p_ref), woa_ref, wob_ref, wout_ref, gple_ref, wpg_ref, wple_ref)
    o_ref[...] = out.reshape(o_ref.shape)


def _weight_specs(d, ple):
    return [_const_spec((512, d)), _const_spec((512, d)), _const_spec((d, d)),
            _const_spec((1, d)), _const_spec((d, d)), _const_spec((ple, d))]


def _prompt_layer(x, p, proj_args, tat, tbt, sink, weights):
    gin, w_in, gqa, gka, gqb, gkb = proj_args
    bsz, seq, d = x.shape
    tq = ATTN_TILE
    n_hist = BAND // tq
    n_blocks = seq // tq
    cur = lambda w: pl.BlockSpec((1, STEP_BLOCKS * tq, w), lambda b, i: (b, i, 0))
    nxt = lambda w: pl.BlockSpec((1, tq, w), lambda b, i: (
        b, jnp.minimum(STEP_BLOCKS * (i + 1), n_blocks - 1), 0))
    tail = lambda r, w: pl.BlockSpec((1, r, w), lambda b, i: (b, 0, 0))
    assert (n_blocks - n_hist) % STEP_BLOCKS == 0
    btail = pl.BlockSpec((1, tq, 512), lambda b, i: (
        b, jnp.clip(STEP_BLOCKS * i + 1 - (n_blocks - n_hist), 0, n_hist - 1), 0))
    return pl.pallas_call(
        _prompt_kernel,
        grid=(bsz, n_blocks // STEP_BLOCKS),
        in_specs=[cur(d), nxt(d), cur(p.shape[-1]), _const_spec((1, d)), _const_spec(w_in.shape),
                  _const_spec((1, LANES)), _const_spec((1, LANES)),
                  _const_spec((1, LANES)), _const_spec((1, LANES)),
                  _const_spec(tat.shape), _const_spec(tbt.shape),
                  pl.BlockSpec(memory_space=pltpu.SMEM)]
                 + _weight_specs(d, p.shape[-1]),
        out_specs=[cur(d), tail(WIN, 128), tail(WIN, 128), btail, btail],
        out_shape=[jax.ShapeDtypeStruct((bsz, seq, d), _F32),
                   jax.ShapeDtypeStruct((bsz, WIN, 128), _F32),
                   jax.ShapeDtypeStruct((bsz, WIN, 128), _F32),
                   jax.ShapeDtypeStruct((bsz, BAND, 512), _F32),
                   jax.ShapeDtypeStruct((bsz, BAND, 512), _F32)],
        scratch_shapes=[pltpu.VMEM((2, 512, tq), _BF16), pltpu.VMEM((2, 512, tq), _BF16),
                        pltpu.VMEM((2, tq, _C_END - _C_ZB + _C_QB - _C_ZA), _BF16),
                        pltpu.VMEM((n_hist + 2, tq, 512), _BF16), pltpu.VMEM((n_hist + 2, 512, tq), _BF16),
                        pltpu.VMEM((tq, 512), _BF16), pltpu.VMEM((512, tq), _BF16),
                        pltpu.VMEM((WIN + tq, 256), _BF16), pltpu.VMEM((LANES, WIN + tq), _BF16),
                        pltpu.VMEM((tq, 256), _BF16), pltpu.VMEM((LANES, tq), _BF16)],
        compiler_params=pltpu.CompilerParams(
            dimension_semantics=("arbitrary", "arbitrary"),
            vmem_limit_bytes=VMEM_LIMIT_BYTES),
        name="prompt_layer",
    )(x, x, p, gin, w_in, gqa, gka, gqb, gkb, tat, tbt, sink, *weights)


def _sample_attend(acts, caches, x, p, ta, tb, sink, weights):
    qa, ka, va, za, qb, kb, vb, zb, ga, gb = acts
    cka, cva, ckb, cvb = caches
    bsz, seq, d = x.shape
    nseq = math.gcd(bsz, SAMPLE_SEQS)
    blk = lambda a: pl.BlockSpec((nseq,) + a.shape[1:], lambda b: (b, 0, 0))
    ins = (qa, qb, cka, ka, cva, va, ckb, kb, cvb, vb, za, zb, ga, gb, x, p)
    ta_spec = _const_spec((ta.shape[0], seq, ta.shape[2]))
    tb_spec = _const_spec((tb.shape[0], seq, tb.shape[2]))
    return pl.pallas_call(
        _sample_attn_kernel,
        grid=(bsz // nseq,),
        in_specs=[blk(a) for a in ins] + [ta_spec, tb_spec, pl.BlockSpec(memory_space=pltpu.SMEM)]
                 + _weight_specs(d, p.shape[-1]),
        out_specs=blk(x),
        out_shape=jax.ShapeDtypeStruct((bsz, seq, d), _F32),
        compiler_params=pltpu.CompilerParams(
            dimension_semantics=("arbitrary",),
            vmem_limit_bytes=VMEM_LIMIT_BYTES),
        name="sample_attn",
    )(*ins, ta, tb, sink, *weights)


def _dup_cache(c):
    return jnp.concatenate([c[:, :, 0], c[:, :, 0], c[:, :, 1], c[:, :, 1]], axis=-1).astype(_BF16)


def kernel(x_prompt, x_sample, cache_k_win, cache_v_win, cache_k_band, cache_v_band, p_prompt, p_sample, g_in, w_in, g_q_win, g_k_win, sink_win, g_q_band, g_k_band, rel_bias_band, w_o_win, w_o_band, w_out, g_ple, w_ple_gate, w_ple):
    depth = w_in.shape[0]
    bsz, seq, d = x_prompt.shape
    dbsz, dseq, _ = x_sample.shape
    assert seq % (STEP_BLOCKS * ATTN_TILE) == 0 and seq >= BAND + ATTN_TILE and dseq == CHUNK
    assert STEP_BLOCKS % 2 == 0
    assert ATTN_TILE % WIN_BLOCK == 0 and WIN_BLOCK == WIN and BAND % ATTN_TILE == 0
    assert BAND_HEADS == (WIN_HEADS // 2) * (ATTN_TILE // WIN_BLOCK)
    assert cache_k_win.shape[2] == WIN and cache_k_band.shape[2] == BAND

    h_p, h_s = x_prompt, x_sample
    outs = [[] for _ in range(8)]
    ta, tat = _win_bias_table(WIN_HEADS)
    for i in range(depth):
        pair = lambda g: jnp.tile(g[i], 2)[None, :]
        proj_args = (g_in[i][None, :], w_in[i].astype(_BF16),
                     pair(g_q_win), pair(g_k_win), pair(g_q_band), pair(g_k_band))
        weights = (w_o_win[i].astype(_BF16), w_o_band[i].astype(_BF16), w_out[i].astype(_BF16),
                   g_ple[i][None, :], w_ple_gate[i].astype(_BF16), w_ple[i].astype(_BF16))
        tb, tbt = _band_bias_table(rel_bias_band[i], ATTN_TILE)

        h_p, kat, vat, kbt, vbt = _prompt_layer(h_p, p_prompt[i], proj_args, tat, tbt, sink_win[i], weights)
        outs[0].append(kat.reshape(bsz, WIN, WIN_KV_HEADS, HEAD_DIM))
        outs[1].append(vat.reshape(bsz, WIN, WIN_KV_HEADS, HEAD_DIM))
        outs[2].append(kbt.reshape(bsz, BAND, BAND_HEADS, HEAD_DIM))
        outs[3].append(vbt.reshape(bsz, BAND, BAND_HEADS, HEAD_DIM))

        ntok = dbsz * dseq
        *acts, kan, van, kbn, vbn = _project(h_s.reshape(1, ntok, d), *proj_args,
                                             tile=ntok, tail_a=ntok, tail_b=ntok)
        acts = [a.reshape(dbsz, dseq, a.shape[-1]) for a in acts]
        caches = (_dup_cache(cache_k_win[i]), _dup_cache(cache_v_win[i]),
                  cache_k_band[i].reshape(dbsz, BAND, -1).astype(_BF16),
                  cache_v_band[i].reshape(dbsz, BAND, -1).astype(_BF16))
        h_s = _sample_attend(acts, caches, h_s, p_sample[i], ta, tb, sink_win[i], weights)
        new = lambda t, nh: t.reshape(dbsz, dseq, nh, HEAD_DIM)
        outs[4].append(jnp.concatenate([cache_k_win[i][:, dseq:], new(kan, WIN_KV_HEADS)], axis=1))
        outs[5].append(jnp.concatenate([cache_v_win[i][:, dseq:], new(van, WIN_KV_HEADS)], axis=1))
        outs[6].append(jnp.concatenate([cache_k_band[i][:, dseq:], new(kbn, BAND_HEADS)], axis=1))
        outs[7].append(jnp.concatenate([cache_v_band[i][:, dseq:], new(vbn, BAND_HEADS)], axis=1))
    return (h_p, h_s) + tuple(jnp.stack(o) for o in outs)
```
